```python
import math, functools
import jax, jax.numpy as jnp
from jax import lax
import numpy as np

D_MODEL = 1024
BATCH = 8
SEQ = 4096
DEPTH = 2

GRID_W = 64
CTX_LEN = 256

ATT_HEADS = 8
ATT_KV_HEADS = 2
GQA_GROUP = ATT_HEADS // ATT_KV_HEADS
HEAD_DIM = 64
ATT_Q_DIM = ATT_HEADS * HEAD_DIM
ATT_KV_DIM = ATT_KV_HEADS * HEAD_DIM
WINDOW = 128
ATT_BLOCK = 128
ATT_SCALE = HEAD_DIM ** -0.5
ROPE_BASE = 10000.0
ROPE_FREQS = HEAD_DIM // 4
RWKV_HEADS = 8
RWKV_HEAD = 64
RWKV_DIM = RWKV_HEADS * RWKV_HEAD
DECAY_LORA = 64
ICLR_LORA = 64
VRES_LORA = 32
GATE_LORA = 128
SHIFT_WIDTH = 3
GN_EPS = 64e-5
RWKV_SPLITS = (RWKV_DIM, RWKV_DIM, RWKV_DIM, DECAY_LORA, ICLR_LORA, GATE_LORA)
RWKV_IN = sum(RWKV_SPLITS)
RWKV_CUTS = tuple(np.cumsum(RWKV_SPLITS)[:-1].tolist())
CHUNK = 128
SGU_GROUPS = 4
SGU_DIM = 512
SGU_GROUP_DIM = SGU_DIM // SGU_GROUPS
N_BRANCH = 3
FFN_HIDDEN = -(-(8 * D_MODEL) // (3 * 256)) * 256
EPS = 1e-6
IN_SPLITS = (ATT_Q_DIM, ATT_KV_DIM, ATT_KV_DIM, RWKV_IN, 2 * SGU_DIM, N_BRANCH * D_MODEL)
IN_DIM = sum(IN_SPLITS)
IN_CUTS = tuple(np.cumsum(IN_SPLITS)[:-1].tolist())

kernel_name = 'hybrid_prefix_dit_attn_rwkv7_sgu'


def rmsnorm(t, gain):
    t32 = t.astype(jnp.float32)
    return (t32 * lax.rsqrt(jnp.mean(t32 * t32, -1, keepdims=True) + EPS) * gain).astype(t.dtype)


def layernorm(t, w, b):
    t32 = t.astype(jnp.float32)
    mu = jnp.mean(t32, -1, keepdims=True)
    var = jnp.mean(jnp.square(t32 - mu), -1, keepdims=True)
    return ((t32 - mu) * lax.rsqrt(var + EPS) * w + b).astype(t.dtype)


def modulate(h, shift, scale):
    return h * (1.0 + scale) + shift


def axial_rope(n):
    rows = n // GRID_W
    row = jnp.repeat(jnp.arange(rows), GRID_W).astype(jnp.float32)
    col = jnp.tile(jnp.arange(GRID_W), rows).astype(jnp.float32)
    inv = ROPE_BASE ** (-jnp.arange(ROPE_FREQS, dtype=jnp.float32) / ROPE_FREQS)
    ang = jnp.concatenate([row[:, None] * inv, col[:, None] * inv], -1)
    return jnp.cos(ang), jnp.sin(ang)


def apply_rope(t, cos, sin):
    t32 = t.astype(jnp.float32)
    t1, t2 = jnp.split(t32, 2, -1)
    cos, sin = cos[None, :, None, :], sin[None, :, None, :]
    return jnp.concatenate([t1 * cos - t2 * sin, t1 * sin + t2 * cos], -1).astype(t.dtype)


def head_rmsnorm(p, n_heads, gain):
    return rmsnorm(p.reshape(p.shape[:-1] + (n_heads, HEAD_DIM)), gain)


def sink_attend(qg, keys, vals, masks, sink):
    B, Q = qg.shape[:2]
    logits = []
    for kt, m in zip(keys, masks):
        s = jnp.einsum('bqhgd,bkhd->bhgqk', qg, kt).astype(jnp.float32) * ATT_SCALE
        if m is not None:
            s = jnp.where(m, s, -jnp.inf)
        logits.append(s)
    sink_col = sink.astype(jnp.float32).reshape(ATT_KV_HEADS, GQA_GROUP, 1, 1)
    logits.append(jnp.broadcast_to(sink_col, (B, ATT_KV_HEADS, GQA_GROUP, Q, 1)))
    p = jax.nn.softmax(jnp.concatenate(logits, -1), axis=-1)
    out, start = None, 0
    for vt in vals:
        n = vt.shape[1]
        o = jnp.einsum('bhgqk,bkhd->bqhgd', p[..., start:start + n].astype(vt.dtype), vt)
        out = o if out is None else out + o
        start += n
    return out


def _band(t, nb):
    B, _, H, hd = t.shape
    tb = jnp.pad(t.reshape(B, nb, ATT_BLOCK, H, hd), ((0, 0), (1, 1), (0, 0), (0, 0), (0, 0)))
    band = jnp.concatenate([tb[:, :-2], tb[:, 1:-1], tb[:, 2:]], axis=2)
    return jnp.moveaxis(band, 1, 0)


def window_attention(q, k, v, kc, vc, sink):
    B, N = q.shape[:2]
    nb = N // ATT_BLOCK
    qb = jnp.moveaxis(q.reshape(B, nb, ATT_BLOCK, ATT_KV_HEADS, GQA_GROUP, HEAD_DIM), 1, 0)
    kb, vb = _band(k, nb), _band(v, nb)
    offs = jnp.arange(3 * ATT_BLOCK) - ATT_BLOCK
    rel = offs[None, :] - jnp.arange(ATT_BLOCK)[:, None]
    key_pos = jnp.arange(nb)[:, None] * ATT_BLOCK + offs[None, :]
    mask = (jnp.abs(rel) <= WINDOW)[None] & ((key_pos >= 0) & (key_pos < N))[:, None, :]

    def block(args):
        qi, ki, vi, mi = args
        return sink_attend(qi, (ki, kc), (vi, vc), (mi, None), sink)

    out = lax.map(block, (qb, kb, vb, mask))
    return jnp.moveaxis(out, 0, 1).reshape(B, N, ATT_Q_DIM)


def context_attention(qc, kc, vc, sink):
    B, L = qc.shape[:2]
    qg = qc.reshape(B, L, ATT_KV_HEADS, GQA_GROUP, HEAD_DIM)
    return sink_attend(qg, (kc,), (vc,), (None,), sink).reshape(B, L, ATT_Q_DIM)


def short_conv(t, w):
    T = t.shape[1]
    half = SHIFT_WIDTH // 2
    tp = jnp.pad(t, ((0, 0), (half, half), (0, 0)))
    return sum(tp[:, i:i + T] * w[i] for i in range(SHIFT_WIDTH))


def _time_major(t):
    t = jnp.stack([t[0], jnp.flip(t[1], axis=1)])
    return jnp.moveaxis(t, 2, 0)


def _rwkv_step(S, inp, emit):
    r, w, k, v, a, b = inp
    sa = jnp.einsum('dbhvk,dbhk->dbhv', S, a)
    S = S * w[..., None, :] + sa[..., :, None] * b[..., None, :] + v[..., :, None] * k[..., None, :]
    return S, (jnp.einsum('dbhvk,dbhk->dbhv', S, r) if emit else None)


def rwkv_time_mix(r, k, v, xw, xa, xg, state0, lp, readout):
    f32 = jnp.float32
    B, T, C = r.shape
    heads = lambda t: t.reshape(t.shape[:-1] + (RWKV_HEADS, RWKV_HEAD))
    kk = heads((k * lp['k_k']).astype(f32))
    kk = kk * lax.rsqrt(jnp.sum(kk * kk, -1, keepdims=True) + 1e-12)
    w_pre = lp['w0'][:, None, None, :] + jnp.einsum('btr,drc->dbtc', jnp.tanh(xw), lp['w_up'])
    decay = jnp.exp(-jnp.exp(-jax.nn.softplus(-w_pre.astype(f32)) - 0.5))
    a = jax.nn.sigmoid((lp['a0'][:, None, None, :] + jnp.einsum('btr,drc->dbtc', xa, lp['a_up'])).astype(f32))
    k_dir = k.astype(f32)[None] * (1.0 + (a - 1.0) * lp['k_a'])
    rh, vh, kdh = heads(r.astype(f32)), heads(v.astype(f32)), heads(k_dir)
    both = lambda t: jnp.stack([t, t])
    seqs = (both(rh), heads(decay), kdh, both(vh), both(-kk), kk[None] * heads(a))
    state, ys = lax.scan(functools.partial(_rwkv_step, emit=readout), state0,
                         tuple(_time_major(t) for t in seqs))
    if not readout:
        return None, state
    y = jnp.moveaxis(ys[:, 0] + jnp.flip(ys[:, 1], axis=0), 0, 1)
    mu = jnp.mean(y, -1, keepdims=True)
    var = jnp.mean(jnp.square(y - mu), -1, keepdims=True)
    y = ((y - mu) * lax.rsqrt(var + GN_EPS)).reshape(B, T, C) * lp['ln_w'] + lp['ln_b']
    bonus = jnp.einsum('bthn,dbthn,hn->bth', rh, kdh, lp['r_k'])[..., None] * vh
    g = jax.nn.sigmoid(xg) @ lp['g_up']
    return ((y + bonus.reshape(B, T, C)) * g).astype(r.dtype), state


def value_residual(v, v_first, h, down, up, bias):
    return v + (v_first - v) * jax.nn.sigmoid(bias + (h @ down) @ up)


def spatial_gating(p_sg, lp):
    u, vg = jnp.split(jax.nn.gelu(p_sg), 2, -1)
    B, T, _ = u.shape
    vn = layernorm(vg, lp['sgu_ln_w'], lp['sgu_ln_b'])
    vc = vn.reshape(B, T // CHUNK, CHUNK, SGU_GROUPS, SGU_GROUP_DIM)
    s = jnp.einsum('gpq,bnqgc->bnpgc', lp['sgu_w'], vc) + lp['sgu_b'].T[None, None, :, :, None]
    return u * s.reshape(B, T, SGU_DIM)


def merge_branches(o_attn, o_rwkv, o_sgu, p_gate, lp):
    ga, gb, gc = jnp.split(jax.nn.sigmoid(p_gate), N_BRANCH, -1)
    m = ga * (o_attn @ lp['w_o_attn']) + gb * (o_rwkv @ lp['w_o_rwkv']) + gc * (o_sgu @ lp['w_o_sgu'])
    return m @ lp['w_out']


def token_mixers(h, hc, lp, rope, v_first, vres, last):
    pq, pk, pv, p_rw, p_sg, p_gt = jnp.split(h @ lp['w_in'], IN_CUTS, -1)
    cq, ck, cv, c_rw, c_sg, c_gt = jnp.split(hc @ lp['w_in'], IN_CUTS, -1)
    B, L = hc.shape[:2]
    q = apply_rope(head_rmsnorm(pq, ATT_HEADS, lp['q_gain']), *rope)
    k = apply_rope(head_rmsnorm(pk, ATT_KV_HEADS, lp['k_gain']), *rope)
    v = pv.reshape(pv.shape[:-1] + (ATT_KV_HEADS, HEAD_DIM))
    kc = head_rmsnorm(ck, ATT_KV_HEADS, lp['k_gain'])
    vc = cv.reshape(B, L, ATT_KV_HEADS, HEAD_DIM)
    o_attn = window_attention(q, k, v, kc, vc, lp['sink'])
    r, kr, vr, xw, xa, xg = jnp.split(short_conv(p_rw, lp['conv']), RWKV_CUTS, -1)
    rc, krc, vrc, xwc, xac, xgc = jnp.split(short_conv(c_rw, lp['conv']), RWKV_CUTS, -1)
    if vres is not None:
        vr = value_residual(vr, v_first[0], h, *vres)
        vrc = value_residual(vrc, v_first[1], hc, *vres)
    state0 = jnp.zeros((2, B, RWKV_HEADS, RWKV_HEAD, RWKV_HEAD), jnp.float32)
    oc_rwkv, state_c = rwkv_time_mix(rc, krc, vrc, xwc, xac, xgc, state0, lp, not last)
    o_rwkv, _ = rwkv_time_mix(r, kr, vr, xw, xa, xg, state_c, lp, True)
    o_sgu = spatial_gating(p_sg, lp)
    mix = merge_branches(o_attn, o_rwkv, o_sgu, p_gt, lp)
    mix_c = None
    if not last:
        qc = head_rmsnorm(cq, ATT_HEADS, lp['q_gain'])
        mix_c = merge_branches(context_attention(qc, kc, vc, lp['sink']), oc_rwkv,
                               spatial_gating(c_sg, lp), c_gt, lp)
    return mix, mix_c, (vr, vrc)


def swiglu(h, w1, w3, w2):
    return (jax.nn.silu(h @ w1) * (h @ w3)) @ w2


def setup_inputs(seed: int = 0) -> dict:
    key = jax.random.key(seed)
    ks = list(jax.random.split(key, 48))
    nrm = lambda shape, scale: jax.random.normal(ks.pop(), shape, jnp.float32) * scale
    D, L, C, F = D_MODEL, DEPTH, RWKV_DIM, FFN_HIDDEN
    centre = jnp.eye(SHIFT_WIDTH, dtype=jnp.float32)[SHIFT_WIDTH // 2][None, :, None]
    return {
        'x': nrm((BATCH, SEQ, D), 1.0),
        'c': nrm((BATCH, D), 1.0),
        'ctx': nrm((BATCH, CTX_LEN, D), 1.0),
        'c_ctx': nrm((D,), 1.0),
        'w_mod': nrm((L, D, 6 * D), 0.3 * D ** -0.5),
        'b_mod': nrm((L, 6 * D), 0.02),
        'norm_mix': 1.0 + nrm((L, D), 0.05),
        'norm_ffn': 1.0 + nrm((L, D), 0.05),
        'w_in': nrm((L, D, IN_DIM), D ** -0.5),
        'q_gain': 1.0 + nrm((L, HEAD_DIM), 0.1),
        'k_gain': 1.0 + nrm((L, HEAD_DIM), 0.1),
        'attn_sink': nrm((L, ATT_HEADS), 0.5),
        'rwkv_conv': centre + nrm((L, SHIFT_WIDTH, RWKV_IN), 0.2),
        'rwkv_w0': jax.random.uniform(ks.pop(), (L, 2, C), jnp.float32, -6.0, 0.0),
        'rwkv_w_up': nrm((L, 2, DECAY_LORA, C), 0.5 * DECAY_LORA ** -0.5),
        'rwkv_a0': nrm((L, 2, C), 0.1),
        'rwkv_a_up': nrm((L, 2, ICLR_LORA, C), 0.5 * ICLR_LORA ** -0.5),
        'rwkv_k_k': 0.85 + nrm((L, C), 0.05),
        'rwkv_k_a': 1.0 + nrm((L, C), 0.05),
        'rwkv_r_k': nrm((L, RWKV_HEADS, RWKV_HEAD), 0.1),
        'rwkv_g_up': nrm((L, GATE_LORA, C), GATE_LORA ** -0.5),
        'rwkv_ln_w': 1.0 + nrm((L, C), 0.05),
        'rwkv_ln_b': nrm((L, C), 0.02),
        'rwkv_vres_down': nrm((L - 1, D, VRES_LORA), D ** -0.5),
        'rwkv_vres_up': nrm((L - 1, VRES_LORA, C), VRES_LORA ** -0.5),
        'rwkv_vres_b': nrm((L - 1, C), 0.1),
        'sgu_ln_w': 1.0 + nrm((L, SGU_DIM), 0.05),
        'sgu_ln_b': nrm((L, SGU_DIM), 0.02),
        'sgu_w': nrm((L, SGU_GROUPS, CHUNK, CHUNK), CHUNK ** -0.5),
        'sgu_b': 1.0 + nrm((L, SGU_GROUPS, CHUNK), 0.1),
        'w_o_attn': nrm((L, ATT_Q_DIM, D), ATT_Q_DIM ** -0.5),
        'w_o_rwkv': nrm((L, C, D), C ** -0.5),
        'w_o_sgu': nrm((L, SGU_DIM, D), SGU_DIM ** -0.5),
        'w_out': nrm((L, D, D), D ** -0.5),
        'ffn_w1': nrm((L, D, F), D ** -0.5),
        'ffn_w3': nrm((L, D, F), D ** -0.5),
        'ffn_w2': nrm((L, F, D), F ** -0.5),
    }


def reference(x, c, ctx, c_ctx, w_mod, b_mod, norm_mix, norm_ffn, w_in, q_gain, k_gain, attn_sink,
              rwkv_conv, rwkv_w0, rwkv_w_up, rwkv_a0, rwkv_a_up, rwkv_k_k, rwkv_k_a, rwkv_r_k, rwkv_g_up,
              rwkv_ln_w, rwkv_ln_b, rwkv_vres_down, rwkv_vres_up, rwkv_vres_b,
              sgu_ln_w, sgu_ln_b, sgu_w, sgu_b, w_o_attn, w_o_rwkv, w_o_sgu, w_out,
              ffn_w1, ffn_w3, ffn_w2):
    rope = axial_rope(x.shape[1])
    xc = ctx
    v_first = None
    for l in range(DEPTH):
        last = l == DEPTH - 1
        lp = {
            'w_in': w_in[l], 'q_gain': q_gain[l], 'k_gain': k_gain[l], 'sink': attn_sink[l],
            'conv': rwkv_conv[l], 'w0': rwkv_w0[l], 'w_up': rwkv_w_up[l], 'a0': rwkv_a0[l],
            'a_up': rwkv_a_up[l], 'k_k': rwkv_k_k[l], 'k_a': rwkv_k_a[l], 'r_k': rwkv_r_k[l],
            'g_up': rwkv_g_up[l], 'ln_w': rwkv_ln_w[l], 'ln_b': rwkv_ln_b[l],
            'sgu_ln_w': sgu_ln_w[l], 'sgu_ln_b': sgu_ln_b[l], 'sgu_w': sgu_w[l], 'sgu_b': sgu_b[l],
            'w_o_attn': w_o_attn[l], 'w_o_rwkv': w_o_rwkv[l], 'w_o_sgu': w_o_sgu[l], 'w_out': w_out[l],
        }
        vres = None if l == 0 else (rwkv_vres_down[l - 1], rwkv_vres_up[l - 1], rwkv_vres_b[l - 1])
        mod = jax.nn.silu(c) @ w_mod[l] + b_mod[l]
        mod_c = jax.nn.silu(c_ctx) @ w_mod[l] + b_mod[l]
        sh1, sc1, g1, sh2, sc2, g2 = jnp.split(mod[:, None, :], 6, axis=-1)
        csh1, csc1, cg1, csh2, csc2, cg2 = jnp.split(mod_c[None, None, :], 6, axis=-1)
        h = modulate(rmsnorm(x, norm_mix[l]), sh1, sc1)
        hc = modulate(rmsnorm(xc, norm_mix[l]), csh1, csc1)
        mix, mix_c, v_pair = token_mixers(h, hc, lp, rope, v_first, vres, last)
        if l == 0:
            v_first = v_pair
        x = x + g1 * mix
        h = modulate(rmsnorm(x, norm_ffn[l]), sh2, sc2)
        x = x + g2 * swiglu(h, ffn_w1[l], ffn_w3[l], ffn_w2[l])
        if not last:
            xc = xc + cg1 * mix_c
            hc = modulate(rmsnorm(xc, norm_ffn[l]), csh2, csc2)
            xc = xc + cg2 * swiglu(hc, ffn_w1[l], ffn_w3[l], ffn_w2[l])
    return x
```

```python
import functools
import math

import jax
import jax.numpy as jnp
import numpy as np
from jax import lax
from jax.experimental import pallas as pl
from jax.experimental.pallas import tpu as pltpu

HEAD_DIM = 64
ATT_HEADS = 8
ATT_KV_HEADS = 2
GQA_GROUP = ATT_HEADS // ATT_KV_HEADS
ATT_Q_DIM = ATT_HEADS * HEAD_DIM
ATT_KV_DIM = ATT_KV_HEADS * HEAD_DIM
ATT_BLOCK = 128
ATT_SCALE = HEAD_DIM ** -0.5
ROPE_BASE = 10000.0
ROPE_FREQS = HEAD_DIM // 4
GRID_W = 64
RWKV_HEADS = 8
RWKV_HEAD = 64
RWKV_DIM = RWKV_HEADS * RWKV_HEAD
DECAY_LORA = 64
ICLR_LORA = 64
VRES_LORA = 32
GATE_LORA = 128
RWKV_IN = 3 * RWKV_DIM + DECAY_LORA + ICLR_LORA + GATE_LORA
GN_EPS = 64e-5
SGU_CHUNK = 128
SGU_GROUPS = 4
SGU_DIM = 512
N_BRANCH = 3
EPS = 1e-6

LANES = 128
TOKEN_TILE = 256
RWKV_CHUNK = 64
VMEM_LIMIT = 56 * 1024 * 1024

BF16 = jnp.bfloat16
F32 = jnp.float32


def _cparams(sem):
    return pltpu.CompilerParams(dimension_semantics=sem, vmem_limit_bytes=VMEM_LIMIT)


def _dot(a, b):
    return jnp.dot(a.astype(BF16), b.astype(BF16), preferred_element_type=F32)


def _split2(x):
    hi = x.astype(BF16)
    lo = (x - hi.astype(F32)).astype(BF16)
    return hi, lo


def _dot_hi_rhs_exact(x, m_bf16):
    hi, lo = _split2(x)
    return (jnp.dot(hi, m_bf16, preferred_element_type=F32)
            + jnp.dot(lo, m_bf16, preferred_element_type=F32))


def _bmm(a, b, dims, passes):
    dn = (dims, ((0,), (0,)))
    dg = lambda p, q: lax.dot_general(p, q, dn, preferred_element_type=F32)
    if passes == 1:
        return dg(a.astype(BF16), b.astype(BF16))
    ah, al = _split2(a)
    bh, bl = _split2(b)
    return dg(ah, bh) + dg(ah, bl) + dg(al, bh)


_NN = ((2,), (1,))
_NT = ((2,), (2,))
_TN = ((1,), (1,))


def _rms_mod(x, gain, shift, scale):
    ms = jnp.mean(x * x, axis=-1, keepdims=True)
    return (x * lax.rsqrt(ms + EPS) * gain) * (1.0 + scale) + shift


def _mod_kernel(c_ref, w_ref, b_ref, o_ref):
    c = c_ref[...]
    o_ref[...] = _dot(c * jax.nn.sigmoid(c), w_ref[...]) + b_ref[...]


def _modulation(cc, w_mod, b_mod):
    nl, d, n6 = w_mod.shape
    rows = cc.shape[0]
    tn = 1536
    return pl.pallas_call(
        _mod_kernel,
        grid=(nl, n6 // tn),
        in_specs=[
            pl.BlockSpec((rows, d), lambda l, j: (0, 0)),
            pl.BlockSpec((None, d, tn), lambda l, j: (l, 0, j)),
            pl.BlockSpec((None, 1, tn), lambda l, j: (l, 0, j)),
        ],
        out_specs=pl.BlockSpec((None, rows, tn), lambda l, j: (l, 0, j)),
        out_shape=jax.ShapeDtypeStruct((nl, rows, n6), F32),
        compiler_params=_cparams(("parallel", "parallel")),
        name="modulation",
    )(cc, w_mod, b_mod.reshape(nl, 1, n6))


def _proj_in_kernel(has_vres, x_ref, mod_ref, gain_ref, w_ref, cos_ref, sin_ref, qg_ref, kg_ref, gm_ref,
                    q_ref, kv_ref, rw_ref, sg_ref, gt_ref, *maybe_hd_ref):
    x = x_ref[...]
    mod = mod_ref[...]
    hb = _rms_mod(x, gain_ref[...], mod[0:1], mod[1:2]).astype(BF16)
    cos = cos_ref[...]
    sin = sin_ref[...]
    lane = lax.broadcasted_iota(jnp.int32, cos.shape, 1)
    first_half = (lane % HEAD_DIM) < (HEAD_DIM // 2)

    def norm_rope(p, gain):
        ms = _dot_hi_rhs_exact(p * p, gm_ref[...])
        t = p * lax.rsqrt(ms + EPS) * gain
        swapped = jnp.where(first_half, pltpu.roll(t, LANES - HEAD_DIM // 2, 1), pltpu.roll(t, HEAD_DIM // 2, 1))
        return t * cos + swapped * sin

    col = 0
    for j in range(ATT_Q_DIM // LANES):
        p = jnp.dot(hb, w_ref[:, col:col + LANES], preferred_element_type=F32)
        q_ref[:, j * LANES:(j + 1) * LANES] = norm_rope(p, qg_ref[...])
        col += LANES
    p = jnp.dot(hb, w_ref[:, col:col + LANES], preferred_element_type=F32)
    kv_ref[:, 0:LANES] = norm_rope(p, kg_ref[...])
    col += LANES
    kv_ref[:, LANES:2 * LANES] = jnp.dot(hb, w_ref[:, col:col + LANES], preferred_element_type=F32)
    col += LANES
    step = 256
    for j in range(0, RWKV_IN, step):
        rw_ref[:, j:j + step] = jnp.dot(hb, w_ref[:, col + j:col + j + step], preferred_element_type=F32)
    col += RWKV_IN
    for j in range(0, 2 * SGU_DIM, step):
        p = jnp.dot(hb, w_ref[:, col + j:col + j + step], preferred_element_type=F32)
        sg_ref[:, j:j + step] = jax.nn.gelu(p)
    col += 2 * SGU_DIM
    d_model = x.shape[-1]
    for j in range(0, N_BRANCH * d_model, step):
        p = jnp.dot(hb, w_ref[:, col + j:col + j + step], preferred_element_type=F32)
        gt_ref[:, j:j + step] = jax.nn.sigmoid(p)
    col += N_BRANCH * d_model
    if has_vres:
        maybe_hd_ref[0][...] = jnp.dot(hb, w_ref[:, col:col + LANES], preferred_element_type=F32)


def _proj_in(xs, modt, gain, w_ext, cos2, sin2, qg, kg, gmean, n_ctx_tiles, has_vres):
    b, s, d = xs.shape
    tm = TOKEN_TILE
    nt = s // tm
    wcols = w_ext.shape[1]
    tok = lambda width: pl.BlockSpec((None, tm, width), lambda bi, i: (bi, i, 0))
    full = lambda a: pl.BlockSpec(a.shape, lambda bi, i: (0,) * a.ndim)
    out_shapes = [
        jax.ShapeDtypeStruct((b, s, ATT_Q_DIM), F32),
        jax.ShapeDtypeStruct((b, s, 2 * ATT_KV_DIM), F32),
        jax.ShapeDtypeStruct((b, s, RWKV_IN), F32),
        jax.ShapeDtypeStruct((b, s, 2 * SGU_DIM), F32),
        jax.ShapeDtypeStruct((b, s, N_BRANCH * d), F32),
    ]
    out_specs = [tok(ATT_Q_DIM), tok(2 * ATT_KV_DIM), tok(RWKV_IN), tok(2 * SGU_DIM), tok(N_BRANCH * d)]
    if has_vres:
        out_shapes.append(jax.ShapeDtypeStruct((b, s, LANES), F32))
        out_specs.append(tok(LANES))
    return pl.pallas_call(
        functools.partial(_proj_in_kernel, has_vres),
        grid=(b, nt),
        in_specs=[
            tok(d),
            pl.BlockSpec((None, None, 6, d), lambda bi, i: (bi, jnp.where(i < n_ctx_tiles, 0, 1), 0, 0)),
            full(gain),
            pl.BlockSpec((d, wcols), lambda bi, i: (0, 0)),
            pl.BlockSpec((tm, LANES), lambda bi, i: (i, 0)),
            pl.BlockSpec((tm, LANES), lambda bi, i: (i, 0)),
            full(qg), full(kg), full(gmean),
        ],
        out_specs=out_specs,
        out_shape=out_shapes,
        compiler_params=_cparams(("parallel", "parallel")),
        name="proj_in",
    )(xs, modt, gain, w_ext, cos2, sin2, qg, kg, gmean)


def _attn_kernel(n_ctx_blocks, n_blocks, sink_ref, q_ref, kvc_ref, kvp_ref, kvo_ref, kvn_ref, o_ref):
    j = pl.program_id(1)
    blk = ATT_BLOCK
    n_ctx = kvc_ref.shape[0]
    is_lat = j >= n_ctx_blocks
    ok_prev = j >= n_ctx_blocks + 1
    ok_next = jnp.logical_and(is_lat, j <= n_blocks - 2)
    rows = GQA_GROUP * blk
    nkeys = n_ctx + 3 * blk
    qq = lax.broadcasted_iota(jnp.int32, (rows, nkeys), 0) % blk
    kcol = lax.broadcasted_iota(jnp.int32, (rows, nkeys), 1)
    kk = (kcol - n_ctx) % blk
    off_prev = jnp.where(ok_prev, 0, blk)
    off_next = jnp.where(ok_next, 0, blk)
    own_end = jnp.where(is_lat, n_ctx + 2 * blk, 0)
    in_ctx = kcol < n_ctx
    in_prev = jnp.logical_and(kcol >= n_ctx, kcol < n_ctx + blk)
    in_own = jnp.logical_and(kcol >= n_ctx + blk, kcol < own_end)
    in_next = kcol >= n_ctx + 2 * blk
    mask = (in_ctx
            | (in_prev & (kk >= qq + off_prev))
            | in_own
            | (in_next & (kk <= qq - off_next)))
    rgrp = lax.broadcasted_iota(jnp.int32, (rows, 1), 0) // blk
    q = q_ref[...]
    kvc, kvp, kvo, kvn = kvc_ref[...], kvp_ref[...], kvo_ref[...], kvn_ref[...]
    outs = []
    for kh in range(ATT_KV_HEADS):
        h0 = kh * GQA_GROUP
        qh = jnp.concatenate([q[:, (h0 + g) * HEAD_DIM:(h0 + g + 1) * HEAD_DIM] for g in range(GQA_GROUP)], axis=0)
        ksl = slice(kh * HEAD_DIM, (kh + 1) * HEAD_DIM)
        vsl = slice(ATT_KV_DIM + kh * HEAD_DIM, ATT_KV_DIM + (kh + 1) * HEAD_DIM)
        keys = jnp.concatenate([kvc[:, ksl], kvp[:, ksl], kvo[:, ksl], kvn[:, ksl]], axis=0)
        vals = jnp.concatenate([kvc[:, vsl], kvp[:, vsl], kvo[:, vsl], kvn[:, vsl]], axis=0)
        s = lax.dot_general(qh.astype(BF16), keys.astype(BF16), (((1,), (1,)), ((), ())),
                            preferred_element_type=F32) * ATT_SCALE
        s = jnp.where(mask, s, -jnp.inf)
        sink = jnp.zeros((rows, 1), F32)
        for g in range(GQA_GROUP):
            sink = jnp.where(rgrp == g, sink_ref[h0 + g], sink)
        m = jnp.maximum(jnp.max(s, axis=-1, keepdims=True), sink)
        p = jnp.exp(s - m)
        denom = jnp.sum(p, axis=-1, keepdims=True) + jnp.exp(sink - m)
        o = _dot(p, vals) / denom
        outs.extend(o[g * blk:(g + 1) * blk] for g in range(GQA_GROUP))
    o_ref[...] = jnp.concatenate(outs, axis=1)


def _attention(q, kv, sink, n_ctx):
    b, s, _ = q.shape
    blk = ATT_BLOCK
    nb = s // blk
    ncb = n_ctx // blk
    kvw = kv.shape[-1]
    return pl.pallas_call(
        functools.partial(_attn_kernel, ncb, nb),
        grid=(b, nb),
        in_specs=[
            pl.BlockSpec(memory_space=pltpu.SMEM),
            pl.BlockSpec((None, blk, ATT_Q_DIM), lambda bi, j: (bi, j, 0)),
            pl.BlockSpec((None, n_ctx, kvw), lambda bi, j: (bi, 0, 0)),
            pl.BlockSpec((None, blk, kvw), lambda bi, j: (bi, jnp.maximum(j - 1, 0), 0)),
            pl.BlockSpec((None, blk, kvw), lambda bi, j: (bi, j, 0)),
            pl.BlockSpec((None, blk, kvw), lambda bi, j: (bi, jnp.minimum(j + 1, nb - 1), 0)),
        ],
        out_specs=pl.BlockSpec((None, blk, ATT_Q_DIM), lambda bi, j: (bi, j, 0)),
        out_shape=jax.ShapeDtypeStruct((b, s, ATT_Q_DIM), F32),
        compiler_params=_cparams(("parallel", "parallel")),
        name="window_attention",
    )(sink, q, kv, kv, kv, kv)


def _rwkv_tile(d, i, n_tiles):
    return jnp.where(d == 0, i, jnp.where(i == 0, 0, n_tiles - i))


def _rwkv_kernel(has_vres, emit_v, n_ctx_tiles, n_tiles, passes, *refs):
    it = iter(refs)
    rw_ref, rwp_ref, rwn_ref = next(it), next(it), next(it)
    vf_ref = hd_ref = vup_ref = vb_ref = None
    if has_vres:
        vf_ref, hd_ref = next(it), next(it)
    conv_ref, wa_ref, w0a0_ref, kkw_ref, kaw_ref, rkw_ref, gup_ref = (next(it) for _ in range(7))
    if has_vres:
        vup_ref, vb_ref = next(it), next(it)
    gsum_ref, tri_ref = next(it), next(it)
    y_ref, aux_ref = next(it), next(it)
    v_out_ref = next(it) if emit_v else None
    st_ref, fa_ref, fr_ref, fb_ref, fk_ref, fbh_ref, fkh_ref, fv_ref, fel_ref, fy_ref = (next(it) for _ in range(10))

    d = pl.program_id(1)
    i = pl.program_id(2)
    t = _rwkv_tile(d, i, n_tiles)
    tm = rw_ref.shape[0]
    c_dim = RWKV_DIM
    hn = RWKV_HEAD
    lc = RWKV_CHUNK
    n_chunks = tm // lc

    @pl.when(i == 0)
    def _():
        st_ref[...] = jnp.zeros_like(st_ref)

    x = rw_ref[...]
    ok_prev = jnp.logical_and(t != 0, t != n_ctx_tiles)
    ok_next = jnp.logical_and(t != n_ctx_tiles - 1, t != n_tiles - 1)
    halo_p = jnp.where(ok_prev, rwp_ref[7:8, :], 0.0)
    halo_n = jnp.where(ok_next, rwn_ref[0:1, :], 0.0)
    row = lax.broadcasted_iota(jnp.int32, x.shape, 0)
    xp = jnp.where(row == 0, halo_p, pltpu.roll(x, 1, 0))
    xn = jnp.where(row == tm - 1, halo_n, pltpu.roll(x, tm - 1, 0))
    cw = conv_ref[...]
    cv = (xp * cw[0:1] + x * cw[1:2]) + xn * cw[2:3]
    r = cv[:, 0:c_dim]
    k = cv[:, c_dim:2 * c_dim]
    v = cv[:, 2 * c_dim:3 * c_dim]
    xwa = cv[:, 3 * c_dim:3 * c_dim + LANES]
    xg = cv[:, 3 * c_dim + LANES:3 * c_dim + 2 * LANES]

    if emit_v:
        v_out_ref[...] = v
    if has_vres:
        gate = jax.nn.sigmoid(vb_ref[...] + _dot(hd_ref[...], vup_ref[...]))
        v = v + (vf_ref[...] - v) * gate

    gsum = gsum_ref[...]
    kk = k * kkw_ref[...]
    kk = kk * lax.rsqrt(_dot_hi_rhs_exact(kk * kk, gsum) + 1e-12)
    lane = lax.broadcasted_iota(jnp.int32, xwa.shape, 1)
    lora_in = jnp.where(lane < DECAY_LORA, jnp.tanh(xwa), xwa)
    pre = _dot(lora_in, wa_ref[...]) + w0a0_ref[...]
    w_pre = pre[:, 0:c_dim]
    a_sig = jax.nn.sigmoid(pre[:, c_dim:2 * c_dim])
    lw = -jnp.exp(-jax.nn.softplus(-w_pre) - 0.5)
    kd = k * (1.0 + (a_sig - 1.0) * kaw_ref[...])
    bvec = kk * a_sig
    bonus = _dot_hi_rhs_exact(r * kd * rkw_ref[...], gsum) * v
    g = _dot(jax.nn.sigmoid(xg), gup_ref[...])
    aux_ref[:, 0:c_dim] = bonus
    aux_ref[:, c_dim:2 * c_dim] = g

    hi = lw.astype(BF16)
    r1 = lw - hi.astype(F32)
    mid = r1.astype(BF16)
    lo = (r1 - mid.astype(F32)).astype(BF16)
    tri = tri_ref[...]
    cs = (jnp.dot(tri, hi, preferred_element_type=F32) + jnp.dot(tri, mid, preferred_element_type=F32)
          + jnp.dot(tri, lo, preferred_element_type=F32))
    c_inc = cs[0:tm]
    c_tot = cs[tm:2 * tm]
    e_inc = jnp.exp(c_inc)
    e_neg = jnp.exp(-c_inc)
    e_rem = jnp.exp(c_tot - c_inc)
    feats = (
        (fa_ref, -kk * jnp.exp(c_inc - lw)),
        (fr_ref, r * e_inc),
        (fb_ref, bvec * e_neg),
        (fk_ref, kd * e_neg),
        (fbh_ref, bvec * e_rem),
        (fkh_ref, kd * e_rem),
        (fv_ref, v),
        (fel_ref, jnp.exp(c_tot)),
    )
    for ref, val in feats:
        for h in range(RWKV_HEADS):
            ref[h] = val[:, h * hn:(h + 1) * hn]

    ri = lax.broadcasted_iota(jnp.int32, (2 * lc, 2 * lc), 0)
    ci = lax.broadcasted_iota(jnp.int32, (2 * lc, 2 * lc), 1) % lc
    fwd = d == 0
    sgn = jnp.where(fwd, 1, -1)
    amask = jnp.logical_or(jnp.logical_and(ri < lc, sgn * (ri - ci) > 0),
                           jnp.logical_and(ri >= lc, sgn * (ri - lc - ci) >= 0))[None]
    eye = (lax.broadcasted_iota(jnp.int32, (lc, lc), 0) == lax.broadcasted_iota(jnp.int32, (lc, lc), 1))
    eye_f = eye.astype(F32)[None]
    n_doublings = int(math.log2(lc)) - 1

    def chunk_step(n, carry):
        cc = jnp.where(fwd, n, n_chunks - 1 - n)
        rows = pl.ds(pl.multiple_of(cc * lc, lc), lc)
        a_t, r_t, b_t, k_t = fa_ref[:, rows, :], fr_ref[:, rows, :], fb_ref[:, rows, :], fk_ref[:, rows, :]
        b_h, k_h, vv, el = fbh_ref[:, rows, :], fkh_ref[:, rows, :], fv_ref[:, rows, :], fel_ref[:, rows, :]
        a_all = _bmm(jnp.concatenate([a_t, r_t], axis=1), jnp.concatenate([b_t, k_t], axis=1), _NT, passes)
        a_all = jnp.where(amask, a_all, 0.0)
        nmat = a_all[:, 0:lc, 0:lc]
        a_ak = a_all[:, 0:lc, lc:2 * lc]
        tmat = eye_f + nmat
        npow = nmat
        for _ in range(n_doublings):
            npow = _bmm(npow, npow, _NN, passes)
            tmat = tmat + _bmm(tmat, npow, _NN, passes)
        u = _bmm(a_ak, vv, _NN, passes)
        xmat = _bmm(tmat, jnp.concatenate([a_t, u], axis=2), _NN, passes)
        rhs = jnp.concatenate([xmat, jnp.concatenate([jnp.zeros_like(vv), vv], axis=2)], axis=1)
        top = _bmm(a_all[:, lc:2 * lc, :], rhs, _NN, passes)
        bot = _bmm(jnp.concatenate([b_h, k_h], axis=1), rhs, _TN, passes)
        r_p = r_t + top[:, :, 0:lc]
        y_in = top[:, :, lc:2 * lc]
        m_mat = jnp.where(eye[None], el[:, 0:1, :], 0.0) + bot[:, :, 0:lc]
        c_mat = bot[:, :, lc:2 * lc]
        st = st_ref[...]
        res = _bmm(jnp.concatenate([r_p, m_mat], axis=1), st, _NN, passes)
        fy_ref[:, rows, :] = res[:, 0:lc, :] + y_in
        st_ref[...] = res[:, lc:2 * lc, :] + c_mat
        return carry

    lax.fori_loop(0, n_chunks, chunk_step, 0)
    y_ref[...] = jnp.concatenate([fy_ref[h] for h in range(RWKV_HEADS)], axis=1)


def _rwkv(rw, conv_w, wa, w0a0, k_k, k_a, r_k, g_up, gsum, tri, n_ctx_tiles, passes, emit_v, vres=None):
    b, s, _ = rw.shape
    tm = TOKEN_TILE
    nt = s // tm
    c = RWKV_DIM
    halo_blocks = tm // 8
    last_halo = s // 8 - 1
    tile = lambda d, i: _rwkv_tile(d, i, nt)
    tok = lambda width: pl.BlockSpec((None, tm, width), lambda bi, d, i: (bi, tile(d, i), 0))
    full = lambda a: pl.BlockSpec(a.shape, lambda bi, d, i: (0,) * a.ndim)
    per_dir = lambda a: pl.BlockSpec((None,) + a.shape[1:], lambda bi, d, i: (d,) + (0,) * (a.ndim - 1))
    in_specs = [
        tok(RWKV_IN),
        pl.BlockSpec((None, 8, RWKV_IN), lambda bi, d, i: (bi, jnp.maximum(tile(d, i) * halo_blocks - 1, 0), 0)),
        pl.BlockSpec((None, 8, RWKV_IN), lambda bi, d, i: (bi, jnp.minimum((tile(d, i) + 1) * halo_blocks, last_halo), 0)),
    ]
    args = [rw, rw, rw]
    if vres is not None:
        v_first, hd, vup, vb = vres
        in_specs += [tok(c), tok(LANES)]
        args += [v_first, hd]
    in_specs += [full(conv_w), per_dir(wa), per_dir(w0a0), full(k_k), full(k_a), full(r_k), full(g_up)]
    args += [conv_w, wa, w0a0, k_k, k_a, r_k, g_up]
    if vres is not None:
        in_specs += [full(vup), full(vb)]
        args += [vup, vb]
    in_specs += [full(gsum), per_dir(tri)]
    args += [gsum, tri]
    out_tok = lambda width: pl.BlockSpec((None, None, tm, width), lambda bi, d, i: (d, bi, tile(d, i), 0))
    out_specs = [out_tok(c), out_tok(2 * c)]
    out_shape = [jax.ShapeDtypeStruct((2, b, s, c), F32), jax.ShapeDtypeStruct((2, b, s, 2 * c), F32)]
    if emit_v:
        out_specs.append(out_tok(c))
        out_shape.append(jax.ShapeDtypeStruct((2, b, s, c), F32))
    head_buf = pltpu.VMEM((RWKV_HEADS, tm, RWKV_HEAD), F32)
    scratch = [pltpu.VMEM((RWKV_HEADS, RWKV_HEAD, RWKV_HEAD), F32)] + [head_buf] * 9
    return pl.pallas_call(
        functools.partial(_rwkv_kernel, vres is not None, emit_v, n_ctx_tiles, nt, passes),
        grid=(b, 2, nt),
        in_specs=in_specs,
        out_specs=out_specs,
        out_shape=out_shape,
        scratch_shapes=scratch,
        compiler_params=_cparams(("parallel", "parallel", "arbitrary")),
        name="rwkv7_scan",
    )(*args)


def _sgu_kernel(sg_ref, lnw_ref, lnb_ref, w_ref, b_ref, o_ref):
    sg = sg_ref[...]
    u = sg[:, 0:SGU_DIM]
    vg = sg[:, SGU_DIM:2 * SGU_DIM]
    mu = jnp.mean(vg, axis=-1, keepdims=True)
    var = jnp.mean(jnp.square(vg - mu), axis=-1, keepdims=True)
    vn = ((vg - mu) * lax.rsqrt(var + EPS) * lnw_ref[...] + lnb_ref[...]).astype(BF16)
    gd = SGU_DIM // SGU_GROUPS
    for g in range(SGU_GROUPS):
        sl = slice(g * gd, (g + 1) * gd)
        s = jnp.dot(w_ref[g], vn[:, sl], preferred_element_type=F32) + b_ref[g]
        o_ref[:, sl] = u[:, sl] * s


def _sgu(sg, ln_w, ln_b, w_bf16, b_bc):
    b, s, _ = sg.shape
    ck = SGU_CHUNK
    full = lambda a: pl.BlockSpec(a.shape, lambda bi, i: (0,) * a.ndim)
    return pl.pallas_call(
        _sgu_kernel,
        grid=(b, s // ck),
        in_specs=[pl.BlockSpec((None, ck, 2 * SGU_DIM), lambda bi, i: (bi, i, 0)),
                  full(ln_w), full(ln_b), full(w_bf16), full(b_bc)],
        out_specs=pl.BlockSpec((None, ck, SGU_DIM), lambda bi, i: (bi, i, 0)),
        out_shape=jax.ShapeDtypeStruct((b, s, SGU_DIM), F32),
        compiler_params=_cparams(("parallel", "parallel")),
        name="spatial_gating",
    )(sg, ln_w, ln_b, w_bf16, b_bc)


def _merge_kernel(x_ref, mod_ref, oa_ref, y_ref, aux_ref, os_ref, gt_ref, lnw_ref, lnb_ref, gm_ref,
                  woa_ref, wor_ref, wos_ref, wout_ref, o_ref):
    d_model = x_ref.shape[-1]
    c = RWKV_DIM
    y = y_ref[0] + y_ref[1]
    gm = gm_ref[...]
    mu = _dot_hi_rhs_exact(y, gm)
    yc = y - mu
    var = _dot_hi_rhs_exact(yc * yc, gm)
    yn = yc * lax.rsqrt(var + GN_EPS) * lnw_ref[...] + lnb_ref[...]
    bonus = aux_ref[0, :, 0:c] + aux_ref[1, :, 0:c]
    o_rwkv = (yn + bonus) * aux_ref[0, :, c:2 * c]
    m = (gt_ref[:, 0:d_model] * _dot(oa_ref[...], woa_ref[...])
         + gt_ref[:, d_model:2 * d_model] * _dot(o_rwkv, wor_ref[...])
         + gt_ref[:, 2 * d_model:3 * d_model] * _dot(os_ref[...], wos_ref[...]))
    mix = _dot(m, wout_ref[...])
    o_ref[...] = x_ref[...] + mod_ref[2:3, :] * mix


def _merge(xs, modt, o_attn, y, aux, o_sgu, gates, ln_w, ln_b, gmean, woa, wor, wos, wout, n_ctx_tiles):
    b, s, d = xs.shape
    tm = TOKEN_TILE
    tok = lambda width: pl.BlockSpec((None, tm, width), lambda bi, i: (bi, i, 0))
    tok2 = lambda width: pl.BlockSpec((2, None, tm, width), lambda bi, i: (0, bi, i, 0))
    full = lambda a: pl.BlockSpec(a.shape, lambda bi, i: (0,) * a.ndim)
    return pl.pallas_call(
        _merge_kernel,
        grid=(b, s // tm),
        in_specs=[
            tok(d),
            pl.BlockSpec((None, None, 6, d), lambda bi, i: (bi, jnp.where(i < n_ctx_tiles, 0, 1), 0, 0)),
            tok(ATT_Q_DIM), tok2(RWKV_DIM), tok2(2 * RWKV_DIM), tok(SGU_DIM), tok(N_BRANCH * d),
            full(ln_w), full(ln_b), full(gmean), full(woa), full(wor), full(wos), full(wout),
        ],
        out_specs=tok(d),
        out_shape=jax.ShapeDtypeStruct((b, s, d), F32),
        compiler_params=_cparams(("parallel", "parallel")),
        name="merge_out",
    )(xs, modt, o_attn, y, aux, o_sgu, gates, ln_w, ln_b, gmean, woa, wor, wos, wout)


def _ffn_kernel(x_ref, mod_ref, gain_ref, w1_ref, w3_ref, w2_ref, o_ref):
    x = x_ref[...]
    mod = mod_ref[...]
    hb = _rms_mod(x, gain_ref[...], mod[3:4], mod[4:5]).astype(BF16)
    a = jnp.dot(hb, w1_ref[...], preferred_element_type=F32)
    bq = jnp.dot(hb, w3_ref[...], preferred_element_type=F32)
    hid = (a * jax.nn.sigmoid(a)) * bq
    o_ref[...] = x + mod[5:6] * _dot(hid, w2_ref[...])


def _ffn(xs, modt, gain, w1, w3, w2, n_ctx_tiles, first_tile):
    b, s, d = xs.shape
    tm = TOKEN_TILE
    nt = s // tm - first_tile
    full = lambda a: pl.BlockSpec(a.shape, lambda bi, i: (0,) * a.ndim)
    return pl.pallas_call(
        _ffn_kernel,
        grid=(b, nt),
        in_specs=[
            pl.BlockSpec((None, tm, d), lambda bi, i: (bi, i + first_tile, 0)),
            pl.BlockSpec((None, None, 6, d), lambda bi, i: (bi, jnp.where(i + first_tile < n_ctx_tiles, 0, 1), 0, 0)),
            full(gain), full(w1), full(w3), full(w2),
        ],
        out_specs=pl.BlockSpec((None, tm, d), lambda bi, i: (bi, i, 0)),
        out_shape=jax.ShapeDtypeStruct((b, nt * tm, d), F32),
        compiler_params=_cparams(("parallel", "parallel")),
        name="swiglu_ffn",
    )(xs, modt, gain, w1, w3, w2)


def _rope_tables(n_ctx, n_lat):
    rows = n_lat // GRID_W
    row = jnp.repeat(jnp.arange(rows), GRID_W).astype(F32)
    col = jnp.tile(jnp.arange(GRID_W), rows).astype(F32)
    inv = ROPE_BASE ** (-jnp.arange(ROPE_FREQS, dtype=F32) / ROPE_FREQS)
    ang = jnp.concatenate([row[:, None] * inv, col[:, None] * inv], -1)
    cos, sin = jnp.cos(ang), jnp.sin(ang)
    cos = jnp.concatenate([jnp.ones((n_ctx, HEAD_DIM // 2), F32), cos], 0)
    sin = jnp.concatenate([jnp.zeros((n_ctx, HEAD_DIM // 2), F32), sin], 0)
    return jnp.tile(jnp.concatenate([cos, cos], -1), (1, 2)), jnp.tile(jnp.concatenate([-sin, sin], -1), (1, 2))


def _block_diag_ones(n, blk, scale):
    i = np.arange(n)
    return jnp.asarray(((i[:, None] // blk) == (i[None, :] // blk)).astype(np.float32) * scale, dtype=BF16)


def _cumsum_matrices(tm, lc):
    i = np.arange(tm)
    same = (i[:, None] // lc) == (i[None, :] // lc)
    fwd = same & (i[None, :] <= i[:, None])
    bwd = same & (i[None, :] >= i[:, None])
    mats = np.stack([np.concatenate([fwd, same], 0), np.concatenate([bwd, same], 0)]).astype(np.float32)
    return jnp.asarray(mats, dtype=BF16)


def kernel(x, c, ctx, c_ctx, w_mod, b_mod, norm_mix, norm_ffn, w_in, q_gain, k_gain, attn_sink,
           rwkv_conv, rwkv_w0, rwkv_w_up, rwkv_a0, rwkv_a_up, rwkv_k_k, rwkv_k_a, rwkv_r_k, rwkv_g_up,
           rwkv_ln_w, rwkv_ln_b, rwkv_vres_down, rwkv_vres_up, rwkv_vres_b,
           sgu_ln_w, sgu_ln_b, sgu_w, sgu_b, w_o_attn, w_o_rwkv, w_o_sgu, w_out,
           ffn_w1, ffn_w3, ffn_w2):
    b, n_lat, d = x.shape
    n_ctx = ctx.shape[1]
    depth = w_in.shape[0]
    tm = TOKEN_TILE
    assert n_ctx % tm == 0 and n_lat % tm == 0 and n_lat % GRID_W == 0
    n_ctx_tiles = n_ctx // tm
    cdim = RWKV_DIM
    rwkv_passes = 3

    xs = jnp.concatenate([ctx, x], axis=1)
    cos2, sin2 = _rope_tables(n_ctx, n_lat)
    gmean = _block_diag_ones(cdim, HEAD_DIM, 1.0 / HEAD_DIM)
    gmean2 = gmean[:LANES, :LANES]
    gsum = _block_diag_ones(cdim, RWKV_HEAD, 1.0)
    tri = _cumsum_matrices(tm, RWKV_CHUNK)

    rows = -(-(b + 1) // 8) * 8
    cc = jnp.zeros((rows, d), F32).at[:b].set(c).at[b].set(c_ctx)
    mods = _modulation(cc, w_mod, b_mod)

    row2 = lambda a: a.reshape(1, -1)
    v_first = None
    for l in range(depth):
        last = l == depth - 1
        has_vres = l > 0
        mod_lat = mods[l, :b].reshape(b, 1, 6, d)
        mod_ctx = jnp.broadcast_to(mods[l, b].reshape(1, 1, 6, d), (b, 1, 6, d))
        modt = jnp.concatenate([mod_ctx, mod_lat], axis=1)

        w_ext = w_in[l]
        if has_vres:
            down = jnp.pad(rwkv_vres_down[l - 1], ((0, 0), (0, LANES - VRES_LORA)))
            w_ext = jnp.concatenate([w_ext, down], axis=1)
        w_ext = w_ext.astype(BF16)
        qg2 = jnp.tile(q_gain[l], 2).reshape(1, LANES)
        kg2 = jnp.tile(k_gain[l], 2).reshape(1, LANES)
        outs = _proj_in(xs, modt, row2(norm_mix[l]), w_ext, cos2, sin2, qg2, kg2, gmean2, n_ctx_tiles, has_vres)
        q, kv, rw, sg, gates = outs[:5]

        o_attn = _attention(q, kv, attn_sink[l], n_ctx)

        zeros_lora = jnp.zeros((2, DECAY_LORA, cdim), F32)
        wa = jnp.concatenate([jnp.concatenate([rwkv_w_up[l], zeros_lora], axis=2),
                              jnp.concatenate([zeros_lora, rwkv_a_up[l]], axis=2)], axis=1).astype(BF16)
        w0a0 = jnp.concatenate([rwkv_w0[l], rwkv_a0[l]], axis=1).reshape(2, 1, 2 * cdim)
        vres = None
        if has_vres:
            vup = jnp.pad(rwkv_vres_up[l - 1], ((0, LANES - VRES_LORA), (0, 0))).astype(BF16)
            vres = (v_first, outs[5], vup, row2(rwkv_vres_b[l - 1]))
        r_outs = _rwkv(rw, rwkv_conv[l], wa, w0a0, row2(rwkv_k_k[l]), row2(rwkv_k_a[l]), row2(rwkv_r_k[l]),
                       rwkv_g_up[l].astype(BF16), gsum, tri, n_ctx_tiles, rwkv_passes, emit_v=(l == 0), vres=vres)
        y, aux = r_outs[0], r_outs[1]
        if l == 0:
            v_first = r_outs[2][0]

        sgu_bb = jnp.broadcast_to(sgu_b[l][:, :, None], (SGU_GROUPS, SGU_CHUNK, SGU_DIM // SGU_GROUPS))
        o_sgu = _sgu(sg, row2(sgu_ln_w[l]), row2(sgu_ln_b[l]), sgu_w[l].astype(BF16), sgu_bb)

        xs = _merge(xs, modt, o_attn, y, aux, o_sgu, gates, row2(rwkv_ln_w[l]), row2(rwkv_ln_b[l]), gmean,
                    w_o_attn[l].astype(BF16), w_o_rwkv[l].astype(BF16), w_o_sgu[l].astype(BF16),
                    w_out[l].astype(BF16), n_ctx_tiles)
        xs = _ffn(xs, modt, row2(norm_ffn[l]), ffn_w1[l].astype(BF16), ffn_w3[l].astype(BF16),
                  ffn_w2[l].astype(BF16), n_ctx_tiles, first_tile=n_ctx_tiles if last else 0)
    return xs
```

```python
import functools
import math

import jax
import jax.numpy as jnp
import numpy as np
from jax import lax
from jax.experimental import pallas as pl
from jax.experimental.pallas import tpu as pltpu

HEAD_DIM = 64
ATT_HEADS = 8
ATT_KV_HEADS = 2
GQA_GROUP = ATT_HEADS // ATT_KV_HEADS
ATT_Q_DIM = ATT_HEADS * HEAD_DIM
ATT_KV_DIM = ATT_KV_HEADS * HEAD_DIM
ATT_BLOCK = 128
ATT_SCALE = HEAD_DIM ** -0.5
ROPE_BASE = 10000.0
ROPE_FREQS = HEAD_DIM // 4
GRID_W = 64
RWKV_HEADS = 8
RWKV_HEAD = 64
RWKV_DIM = RWKV_HEADS * RWKV_HEAD
DECAY_LORA = 64
ICLR_LORA = 64
VRES_LORA = 32
GATE_LORA = 128
RWKV_IN = 3 * RWKV_DIM + DECAY_LORA + ICLR_LORA + GATE_LORA
GN_EPS = 64e-5
SGU_CHUNK = 128
SGU_GROUPS = 4
SGU_DIM = 512
N_BRANCH = 3
EPS = 1e-6

LANES = 128
TOKEN_TILE = 256
RWKV_CHUNK = 64
MXU_DIM = 256
RWKV_PACK = MXU_DIM // RWKV_HEAD
VMEM_LIMIT = 56 * 1024 * 1024

BF16 = jnp.bfloat16
F32 = jnp.float32


def _cparams(sem):
    return pltpu.CompilerParams(dimension_semantics=sem, vmem_limit_bytes=VMEM_LIMIT)


def _dot(a, b):
    return jnp.dot(a.astype(BF16), b.astype(BF16), preferred_element_type=F32)


def _split2(x):
    hi = x.astype(BF16)
    lo = (x - hi.astype(F32)).astype(BF16)
    return hi, lo


def _dot_hi_rhs_exact(x, m_bf16):
    hi, lo = _split2(x)
    return (jnp.dot(hi, m_bf16, preferred_element_type=F32)
            + jnp.dot(lo, m_bf16, preferred_element_type=F32))


def _bmm(a, b, dims, passes):
    dn = (dims, ((0,), (0,)))
    dg = lambda p, q: lax.dot_general(p, q, dn, preferred_element_type=F32)
    if passes == 1:
        return dg(a.astype(BF16), b.astype(BF16))
    ah, al = _split2(a)
    bh, bl = _split2(b)
    return dg(ah, bh) + dg(ah, bl) + dg(al, bh)


_NN = ((2,), (1,))
_NT = ((2,), (2,))
_TN = ((1,), (1,))


def _rms_mod(x, gain, shift, scale):
    ms = jnp.mean(x * x, axis=-1, keepdims=True)
    return (x * lax.rsqrt(ms + EPS) * gain) * (1.0 + scale) + shift


def _mod_kernel(c_ref, w_ref, b_ref, o_ref):
    c = c_ref[...]
    o_ref[...] = _dot(c * jax.nn.sigmoid(c), w_ref[...]) + b_ref[...]


def _modulation(cc, w_mod, b_mod):
    nl, d, n6 = w_mod.shape
    rows = cc.shape[0]
    tn = 1536
    return pl.pallas_call(
        _mod_kernel,
        grid=(nl, n6 // tn),
        in_specs=[
            pl.BlockSpec((rows, d), lambda l, j: (0, 0)),
            pl.BlockSpec((None, d, tn), lambda l, j: (l, 0, j)),
            pl.BlockSpec((None, 1, tn), lambda l, j: (l, 0, j)),
        ],
        out_specs=pl.BlockSpec((None, rows, tn), lambda l, j: (l, 0, j)),
        out_shape=jax.ShapeDtypeStruct((nl, rows, n6), F32),
        compiler_params=_cparams(("parallel", "parallel")),
        name="modulation",
    )(cc, w_mod, b_mod.reshape(nl, 1, n6))


def _proj_in_kernel(has_vres, x_ref, mod_ref, gain_ref, w_ref, cos_ref, sin_ref, qg_ref, kg_ref, gm_ref,
                    q_ref, kv_ref, rw_ref, sg_ref, gt_ref, *maybe_hd_ref):
    x = x_ref[...]
    mod = mod_ref[...]
    hb = _rms_mod(x, gain_ref[...], mod[0:1], mod[1:2]).astype(BF16)
    cos = cos_ref[...]
    sin = sin_ref[...]
    lane = lax.broadcasted_iota(jnp.int32, cos.shape, 1)
    first_half = (lane % HEAD_DIM) < (HEAD_DIM // 2)

    def norm_rope(p, gain):
        ms = _dot_hi_rhs_exact(p * p, gm_ref[...])
        t = p * lax.rsqrt(ms + EPS) * gain
        swapped = jnp.where(first_half, pltpu.roll(t, LANES - HEAD_DIM // 2, 1), pltpu.roll(t, HEAD_DIM // 2, 1))
        return t * cos + swapped * sin

    col = 0
    for j in range(ATT_Q_DIM // LANES):
        p = jnp.dot(hb, w_ref[:, col:col + LANES], preferred_element_type=F32)
        q_ref[:, j * LANES:(j + 1) * LANES] = norm_rope(p, qg_ref[...])
        col += LANES
    p = jnp.dot(hb, w_ref[:, col:col + LANES], preferred_element_type=F32)
    kv_ref[:, 0:LANES] = norm_rope(p, kg_ref[...])
    col += LANES
    kv_ref[:, LANES:2 * LANES] = jnp.dot(hb, w_ref[:, col:col + LANES], preferred_element_type=F32)
    col += LANES
    step = 256
    for j in range(0, RWKV_IN, step):
        rw_ref[:, j:j + step] = jnp.dot(hb, w_ref[:, col + j:col + j + step], preferred_element_type=F32)
    col += RWKV_IN
    for j in range(0, 2 * SGU_DIM, step):
        p = jnp.dot(hb, w_ref[:, col + j:col + j + step], preferred_element_type=F32)
        sg_ref[:, j:j + step] = jax.nn.gelu(p)
    col += 2 * SGU_DIM
    d_model = x.shape[-1]
    for j in range(0, N_BRANCH * d_model, step):
        p = jnp.dot(hb, w_ref[:, col + j:col + j + step], preferred_element_type=F32)
        gt_ref[:, j:j + step] = jax.nn.sigmoid(p)
    col += N_BRANCH * d_model
    if has_vres:
        maybe_hd_ref[0][...] = jnp.dot(hb, w_ref[:, col:col + LANES], preferred_element_type=F32)


def _proj_in(xs, modt, gain, w_ext, cos2, sin2, qg, kg, gmean, n_ctx_tiles, has_vres):
    b, s, d = xs.shape
    tm = TOKEN_TILE
    nt = s // tm
    wcols = w_ext.shape[1]
    tok = lambda width: pl.BlockSpec((None, tm, width), lambda bi, i: (bi, i, 0))
    full = lambda a: pl.BlockSpec(a.shape, lambda bi, i: (0,) * a.ndim)
    out_shapes = [
        jax.ShapeDtypeStruct((b, s, ATT_Q_DIM), F32),
        jax.ShapeDtypeStruct((b, s, 2 * ATT_KV_DIM), F32),
        jax.ShapeDtypeStruct((b, s, RWKV_IN), F32),
        jax.ShapeDtypeStruct((b, s, 2 * SGU_DIM), F32),
        jax.ShapeDtypeStruct((b, s, N_BRANCH * d), F32),
    ]
    out_specs = [tok(ATT_Q_DIM), tok(2 * ATT_KV_DIM), tok(RWKV_IN), tok(2 * SGU_DIM), tok(N_BRANCH * d)]
    if has_vres:
        out_shapes.append(jax.ShapeDtypeStruct((b, s, LANES), F32))
        out_specs.append(tok(LANES))
    return pl.pallas_call(
        functools.partial(_proj_in_kernel, has_vres),
        grid=(b, nt),
        in_specs=[
            tok(d),
            pl.BlockSpec((None, None, 6, d), lambda bi, i: (bi, jnp.where(i < n_ctx_tiles, 0, 1), 0, 0)),
            full(gain),
            pl.BlockSpec((d, wcols), lambda bi, i: (0, 0)),
            pl.BlockSpec((tm, LANES), lambda bi, i: (i, 0)),
            pl.BlockSpec((tm, LANES), lambda bi, i: (i, 0)),
            full(qg), full(kg), full(gmean),
        ],
        out_specs=out_specs,
        out_shape=out_shapes,
        compiler_params=_cparams(("parallel", "parallel")),
        name="proj_in",
    )(xs, modt, gain, w_ext, cos2, sin2, qg, kg, gmean)


def _attn_kernel(n_ctx_blocks, n_blocks, sink_ref, q_ref, kvc_ref, kvp_ref, kvo_ref, kvn_ref, o_ref):
    j = pl.program_id(1)
    blk = ATT_BLOCK
    n_ctx = kvc_ref.shape[0]
    is_lat = j >= n_ctx_blocks
    ok_prev = j >= n_ctx_blocks + 1
    ok_next = jnp.logical_and(is_lat, j <= n_blocks - 2)
    rows = GQA_GROUP * blk
    nkeys = n_ctx + 3 * blk
    qq = lax.broadcasted_iota(jnp.int32, (rows, nkeys), 0) % blk
    kcol = lax.broadcasted_iota(jnp.int32, (rows, nkeys), 1)
    kk = (kcol - n_ctx) % blk
    off_prev = jnp.where(ok_prev, 0, blk)
    off_next = jnp.where(ok_next, 0, blk)
    own_end = jnp.where(is_lat, n_ctx + 2 * blk, 0)
    in_ctx = kcol < n_ctx
    in_prev = jnp.logical_and(kcol >= n_ctx, kcol < n_ctx + blk)
    in_own = jnp.logical_and(kcol >= n_ctx + blk, kcol < own_end)
    in_next = kcol >= n_ctx + 2 * blk
    mask = (in_ctx
            | (in_prev & (kk >= qq + off_prev))
            | in_own
            | (in_next & (kk <= qq - off_next)))
    rgrp = lax.broadcasted_iota(jnp.int32, (rows, 1), 0) // blk
    q = q_ref[...]
    kvc, kvp, kvo, kvn = kvc_ref[...], kvp_ref[...], kvo_ref[...], kvn_ref[...]
    outs = []
    for kh in range(ATT_KV_HEADS):
        h0 = kh * GQA_GROUP
        qh = jnp.concatenate([q[:, (h0 + g) * HEAD_DIM:(h0 + g + 1) * HEAD_DIM] for g in range(GQA_GROUP)], axis=0)
        ksl = slice(kh * HEAD_DIM, (kh + 1) * HEAD_DIM)
        vsl = slice(ATT_KV_DIM + kh * HEAD_DIM, ATT_KV_DIM + (kh + 1) * HEAD_DIM)
        keys = jnp.concatenate([kvc[:, ksl], kvp[:, ksl], kvo[:, ksl], kvn[:, ksl]], axis=0)
        vals = jnp.concatenate([kvc[:, vsl], kvp[:, vsl], kvo[:, vsl], kvn[:, vsl]], axis=0)
        s = lax.dot_general(qh.astype(BF16), keys.astype(BF16), (((1,), (1,)), ((), ())),
                            preferred_element_type=F32) * ATT_SCALE
        s = jnp.where(mask, s, -jnp.inf)
        sink = jnp.zeros((rows, 1), F32)
        for g in range(GQA_GROUP):
            sink = jnp.where(rgrp == g, sink_ref[h0 + g], sink)
        m = jnp.maximum(jnp.max(s, axis=-1, keepdims=True), sink)
        p = jnp.exp(s - m)
        denom = jnp.sum(p, axis=-1, keepdims=True) + jnp.exp(sink - m)
        o = _dot(p, vals) / denom
        outs.extend(o[g * blk:(g + 1) * blk] for g in range(GQA_GROUP))
    o_ref[...] = jnp.concatenate(outs, axis=1)


def _attention(q, kv, sink, n_ctx):
    b, s, _ = q.shape
    blk = ATT_BLOCK
    nb = s // blk
    ncb = n_ctx // blk
    kvw = kv.shape[-1]
    return pl.pallas_call(
        functools.partial(_attn_kernel, ncb, nb),
        grid=(b, nb),
        in_specs=[
            pl.BlockSpec(memory_space=pltpu.SMEM),
            pl.BlockSpec((None, blk, ATT_Q_DIM), lambda bi, j: (bi, j, 0)),
            pl.BlockSpec((None, n_ctx, kvw), lambda bi, j: (bi, 0, 0)),
            pl.BlockSpec((None, blk, kvw), lambda bi, j: (bi, jnp.maximum(j - 1, 0), 0)),
            pl.BlockSpec((None, blk, kvw), lambda bi, j: (bi, j, 0)),
            pl.BlockSpec((None, blk, kvw), lambda bi, j: (bi, jnp.minimum(j + 1, nb - 1), 0)),
        ],
        out_specs=pl.BlockSpec((None, blk, ATT_Q_DIM), lambda bi, j: (bi, j, 0)),
        out_shape=jax.ShapeDtypeStruct((b, s, ATT_Q_DIM), F32),
        compiler_params=_cparams(("parallel", "parallel")),
        name="window_attention",
    )(sink, q, kv, kv, kv, kv)


def _rwkv_tile(d, i, n_tiles):
    return jnp.where(d == 0, i, jnp.where(i == 0, 0, n_tiles - i))


def _rwkv_kernel(has_vres, emit_v, n_ctx_tiles, n_tiles, passes, *refs):
    it = iter(refs)
    rw_ref, rwp_ref, rwn_ref = next(it), next(it), next(it)
    vf_ref = hd_ref = vup_ref = vb_ref = None
    if has_vres:
        vf_ref, hd_ref = next(it), next(it)
    conv_ref, wa_ref, w0a0_ref, kkw_ref, kaw_ref, rkw_ref, gup_ref = (next(it) for _ in range(7))
    if has_vres:
        vup_ref, vb_ref = next(it), next(it)
    gsum_ref, tri_ref, bdm_ref = next(it), next(it), next(it)
    y_ref, aux_ref = next(it), next(it)
    v_out_ref = next(it) if emit_v else None
    st_ref, fa_ref, fr_ref, fb_ref, fk_ref, fbh_ref, fkh_ref, fv_ref, rm_ref, yin_ref, cm_ref = (next(it) for _ in range(11))

    d = pl.program_id(1)
    i = pl.program_id(2)
    t = _rwkv_tile(d, i, n_tiles)
    tm = rw_ref.shape[0]
    c_dim = RWKV_DIM
    lc = RWKV_CHUNK
    n_chunks = tm // lc
    gw = RWKV_PACK * RWKV_HEAD
    n_groups = c_dim // gw

    @pl.when(i == 0)
    def _():
        st_ref[...] = jnp.zeros_like(st_ref)

    x = rw_ref[...]
    ok_prev = jnp.logical_and(t != 0, t != n_ctx_tiles)
    ok_next = jnp.logical_and(t != n_ctx_tiles - 1, t != n_tiles - 1)
    halo_p = jnp.where(ok_prev, rwp_ref[7:8, :], 0.0)
    halo_n = jnp.where(ok_next, rwn_ref[0:1, :], 0.0)
    row = lax.broadcasted_iota(jnp.int32, x.shape, 0)
    xp = jnp.where(row == 0, halo_p, pltpu.roll(x, 1, 0))
    xn = jnp.where(row == tm - 1, halo_n, pltpu.roll(x, tm - 1, 0))
    cw = conv_ref[...]
    cv = (xp * cw[0:1] + x * cw[1:2]) + xn * cw[2:3]
    r = cv[:, 0:c_dim]
    k = cv[:, c_dim:2 * c_dim]
    v = cv[:, 2 * c_dim:3 * c_dim]
    xwa = cv[:, 3 * c_dim:3 * c_dim + LANES]
    xg = cv[:, 3 * c_dim + LANES:3 * c_dim + 2 * LANES]

    if emit_v:
        v_out_ref[...] = v
    if has_vres:
        gate = jax.nn.sigmoid(vb_ref[...] + _dot(hd_ref[...], vup_ref[...]))
        v = v + (vf_ref[...] - v) * gate

    gsum = gsum_ref[...]
    kk = k * kkw_ref[...]
    kk = kk * lax.rsqrt(_dot_hi_rhs_exact(kk * kk, gsum) + 1e-12)
    lane = lax.broadcasted_iota(jnp.int32, xwa.shape, 1)
    lora_in = jnp.where(lane < DECAY_LORA, jnp.tanh(xwa), xwa)
    pre = _dot(lora_in, wa_ref[...]) + w0a0_ref[...]
    w_pre = pre[:, 0:c_dim]
    a_sig = jax.nn.sigmoid(pre[:, c_dim:2 * c_dim])
    lw = -jnp.exp(-jax.nn.softplus(-w_pre) - 0.5)
    kd = k * (1.0 + (a_sig - 1.0) * kaw_ref[...])
    bvec = kk * a_sig
    bonus = _dot_hi_rhs_exact(r * kd * rkw_ref[...], gsum) * v
    g = _dot(jax.nn.sigmoid(xg), gup_ref[...])
    aux_ref[:, 0:c_dim] = bonus
    aux_ref[:, c_dim:2 * c_dim] = g

    hi = lw.astype(BF16)
    r1 = lw - hi.astype(F32)
    mid = r1.astype(BF16)
    lo = (r1 - mid.astype(F32)).astype(BF16)
    tri = tri_ref[...]
    cs = (jnp.dot(tri, hi, preferred_element_type=F32) + jnp.dot(tri, mid, preferred_element_type=F32)
          + jnp.dot(tri, lo, preferred_element_type=F32))
    c_inc = cs[0:tm]
    c_tot = cs[tm:2 * tm]
    e_inc = jnp.exp(c_inc)
    e_neg = jnp.exp(-c_inc)
    e_rem = jnp.exp(c_tot - c_inc)
    fa_ref[...] = -kk * jnp.exp(c_inc - lw)
    fr_ref[...] = r * e_inc
    fb_ref[...] = bvec * e_neg
    fk_ref[...] = kd * e_neg
    fbh_ref[...] = bvec * e_rem
    fkh_ref[...] = kd * e_rem
    fv_ref[...] = v

    bdm = bdm_ref[...]
    same_head = (lax.broadcasted_iota(jnp.int32, (gw, gw), 0) // RWKV_HEAD
                 == lax.broadcasted_iota(jnp.int32, (gw, gw), 1) // RWKV_HEAD)

    def bd(xb):
        return jnp.concatenate([xb] * RWKV_PACK, axis=0) * bdm

    def fold(full):
        m = jnp.where(same_head, full, 0.0)
        out = m[0:lc]
        for h in range(1, RWKV_PACK):
            out = out + m[h * lc:(h + 1) * lc]
        return out

    mm = lambda p, q: jnp.dot(p, q, preferred_element_type=F32)
    ri = lax.broadcasted_iota(jnp.int32, (2 * lc, 2 * gw), 0)
    ci = lax.broadcasted_iota(jnp.int32, (2 * lc, 2 * gw), 1) % lc
    fwd = d == 0
    sgn = jnp.where(fwd, 1, -1)
    amask = jnp.logical_or(jnp.logical_and(ri < lc, sgn * (ri - ci) > 0),
                           jnp.logical_and(ri >= lc, sgn * (ri - lc - ci) >= 0))
    eye_p = (lax.broadcasted_iota(jnp.int32, (lc, gw), 0)
             == lax.broadcasted_iota(jnp.int32, (lc, gw), 1) % lc).astype(F32)
    n_levels = int(math.log2(lc))

    probs = [(c, g) for c in range(n_chunks) for g in range(n_groups)]
    blk = lambda ref, c, g: ref[c * lc:(c + 1) * lc, g * gw:(g + 1) * gw]
    ab = [blk(fa_ref, c, g).astype(BF16) for c, g in probs]
    vb = [blk(fv_ref, c, g).astype(BF16) for c, g in probs]
    a_all = []
    for j, (c, g) in enumerate(probs):
        lhs = jnp.concatenate([ab[j], blk(fr_ref, c, g).astype(BF16)], axis=0)
        rhs_t = jnp.concatenate([bd(blk(fb_ref, c, g).astype(BF16)), bd(blk(fk_ref, c, g).astype(BF16))], axis=0)
        full = lax.dot_general(lhs, rhs_t, (((1,), (1,)), ((), ())), preferred_element_type=F32)
        a_all.append(jnp.where(amask, full, 0.0))
    a_rb = [m[lc:2 * lc, 0:gw].astype(BF16) for m in a_all]
    a_rk = [m[lc:2 * lc, gw:2 * gw].astype(BF16) for m in a_all]
    t_p = [eye_p + m[0:lc, 0:gw] for m in a_all]
    pb = [m[0:lc, 0:gw].astype(BF16) for m in a_all]
    pb = [mm(p, bd(p)).astype(BF16) for p in pb]
    for lvl in range(1, n_levels):
        if lvl < n_levels - 1:
            res = [mm(jnp.concatenate([p, t.astype(BF16)], axis=0), bd(p)) for p, t in zip(pb, t_p)]
            pb = [m[0:lc].astype(BF16) for m in res]
            t_p = [t + m[lc:2 * lc] for t, m in zip(t_p, res)]
        else:
            t_p = [t + mm(t.astype(BF16), bd(p)) for p, t in zip(pb, t_p)]
    u = [mm(m[0:lc, gw:2 * gw].astype(BF16), bd(v)) for m, v in zip(a_all, vb)]
    x = [mm(t.astype(BF16), jnp.concatenate([bd(a), bd(uu.astype(BF16))], axis=1))
         for t, a, uu in zip(t_p, ab, u)]
    a1b = [m[:, 0:gw].astype(BF16) for m in x]
    u1b = [m[:, gw:2 * gw].astype(BF16) for m in x]
    for j, (c, g) in enumerate(probs):
        yin_ref[c, g] = mm(jnp.concatenate([a_rb[j], a_rk[j]], axis=1),
                           jnp.concatenate([bd(u1b[j]), bd(vb[j])], axis=0))
    for j, (c, g) in enumerate(probs):
        el_c = jnp.exp(c_tot[c * lc:c * lc + 1, g * gw:(g + 1) * gw])
        r1 = blk(fr_ref, c, g) + mm(a_rb[j], bd(a1b[j]))
        bk_t = jnp.concatenate([blk(fbh_ref, c, g), blk(fkh_ref, c, g)], axis=0).T.astype(BF16)
        m_p = fold(mm(bk_t[:, 0:lc], a1b[j])) + eye_p * el_c
        rm_ref[c, g] = jnp.concatenate([r1, m_p], axis=0)
        cm_ref[c, g] = fold(mm(bk_t, jnp.concatenate([u1b[j], vb[j]], axis=0)))

    for n in range(n_chunks):
        cc = jnp.where(fwd, n, n_chunks - 1 - n)
        ys = []
        for g in range(n_groups):
            rm = rm_ref[cc, g]
            st = st_ref[g]
            if passes == 1:
                res = mm(rm.astype(BF16), bd(st.astype(BF16)))
            else:
                rh, rl = _split2(rm)
                sh, sl = _split2(st)
                res = mm(rh, bd(sh)) + mm(rh, bd(sl)) + mm(rl, bd(sh))
            ys.append(res[0:lc] + yin_ref[cc, g])
            st_ref[g] = res[lc:2 * lc] + cm_ref[cc, g]
        y_ref[pl.ds(pl.multiple_of(cc * lc, lc), lc), :] = jnp.concatenate(ys, axis=1)


def _rwkv(rw, conv_w, wa, w0a0, k_k, k_a, r_k, g_up, gsum, tri, bdm, n_ctx_tiles, passes, emit_v, vres=None):
    b, s, _ = rw.shape
    tm = TOKEN_TILE
    nt = s // tm
    c = RWKV_DIM
    halo_blocks = tm // 8
    last_halo = s // 8 - 1
    tile = lambda d, i: _rwkv_tile(d, i, nt)
    tok = lambda width: pl.BlockSpec((None, tm, width), lambda bi, d, i: (bi, tile(d, i), 0))
    full = lambda a: pl.BlockSpec(a.shape, lambda bi, d, i: (0,) * a.ndim)
    per_dir = lambda a: pl.BlockSpec((None,) + a.shape[1:], lambda bi, d, i: (d,) + (0,) * (a.ndim - 1))
    in_specs = [
        tok(RWKV_IN),
        pl.BlockSpec((None, 8, RWKV_IN), lambda bi, d, i: (bi, jnp.maximum(tile(d, i) * halo_blocks - 1, 0), 0)),
        pl.BlockSpec((None, 8, RWKV_IN), lambda bi, d, i: (bi, jnp.minimum((tile(d, i) + 1) * halo_blocks, last_halo), 0)),
    ]
    args = [rw, rw, rw]
    if vres is not None:
        v_first, hd, vup, vb = vres
        in_specs += [tok(c), tok(LANES)]
        args += [v_first, hd]
    in_specs += [full(conv_w), per_dir(wa), per_dir(w0a0), full(k_k), full(k_a), full(r_k), full(g_up)]
    args += [conv_w, wa, w0a0, k_k, k_a, r_k, g_up]
    if vres is not None:
        in_specs += [full(vup), full(vb)]
        args += [vup, vb]
    in_specs += [full(gsum), per_dir(tri), full(bdm)]
    args += [gsum, tri, bdm]
    out_tok = lambda width: pl.BlockSpec((None, None, tm, width), lambda bi, d, i: (d, bi, tile(d, i), 0))
    out_specs = [out_tok(c), out_tok(2 * c)]
    out_shape = [jax.ShapeDtypeStruct((2, b, s, c), F32), jax.ShapeDtypeStruct((2, b, s, 2 * c), F32)]
    if emit_v:
        out_specs.append(out_tok(c))
        out_shape.append(jax.ShapeDtypeStruct((2, b, s, c), F32))
    lc = RWKV_CHUNK
    gw = RWKV_PACK * RWKV_HEAD
    n_groups, n_chunks = c // gw, tm // lc
    feat_buf = pltpu.VMEM((tm, c), F32)
    scratch = ([pltpu.VMEM((n_groups, lc, gw), F32)] + [feat_buf] * 7
               + [pltpu.VMEM((n_chunks, n_groups, 2 * lc, gw), F32)] + [pltpu.VMEM((n_chunks, n_groups, lc, gw), F32)] * 2)
    return pl.pallas_call(
        functools.partial(_rwkv_kernel, vres is not None, emit_v, n_ctx_tiles, nt, passes),
        grid=(b, 2, nt),
        in_specs=in_specs,
        out_specs=out_specs,
        out_shape=out_shape,
        scratch_shapes=scratch,
        compiler_params=_cparams(("parallel", "parallel", "arbitrary")),
        name="rwkv7_scan",
    )(*args)


def _sgu_kernel(sg_ref, lnw_ref, lnb_ref, w_ref, b_ref, o_ref):
    sg = sg_ref[...]
    u = sg[:, 0:SGU_DIM]
    vg = sg[:, SGU_DIM:2 * SGU_DIM]
    mu = jnp.mean(vg, axis=-1, keepdims=True)
    var = jnp.mean(jnp.square(vg - mu), axis=-1, keepdims=True)
    vn = ((vg - mu) * lax.rsqrt(var + EPS) * lnw_ref[...] + lnb_ref[...]).astype(BF16)
    gd = SGU_DIM // SGU_GROUPS
    for g in range(SGU_GROUPS):
        sl = slice(g * gd, (g + 1) * gd)
        s = jnp.dot(w_ref[g], vn[:, sl], preferred_element_type=F32) + b_ref[g]
        o_ref[:, sl] = u[:, sl] * s


def _sgu(sg, ln_w, ln_b, w_bf16, b_bc):
    b, s, _ = sg.shape
    ck = SGU_CHUNK
    full = lambda a: pl.BlockSpec(a.shape, lambda bi, i: (0,) * a.ndim)
    return pl.pallas_call(
        _sgu_kernel,
        grid=(b, s // ck),
        in_specs=[pl.BlockSpec((None, ck, 2 * SGU_DIM), lambda bi, i: (bi, i, 0)),
                  full(ln_w), full(ln_b), full(w_bf16), full(b_bc)],
        out_specs=pl.BlockSpec((None, ck, SGU_DIM), lambda bi, i: (bi, i, 0)),
        out_shape=jax.ShapeDtypeStruct((b, s, SGU_DIM), F32),
        compiler_params=_cparams(("parallel", "parallel")),
        name="spatial_gating",
    )(sg, ln_w, ln_b, w_bf16, b_bc)


def _merge_kernel(x_ref, mod_ref, oa_ref, y_ref, aux_ref, os_ref, gt_ref, lnw_ref, lnb_ref, gm_ref,
                  woa_ref, wor_ref, wos_ref, wout_ref, o_ref):
    d_model = x_ref.shape[-1]
    c = RWKV_DIM
    y = y_ref[0] + y_ref[1]
    gm = gm_ref[...]
    mu = _dot_hi_rhs_exact(y, gm)
    yc = y - mu
    var = _dot_hi_rhs_exact(yc * yc, gm)
    yn = yc * lax.rsqrt(var + GN_EPS) * lnw_ref[...] + lnb_ref[...]
    bonus = aux_ref[0, :, 0:c] + aux_ref[1, :, 0:c]
    o_rwkv = (yn + bonus) * aux_ref[0, :, c:2 * c]
    m = (gt_ref[:, 0:d_model] * _dot(oa_ref[...], woa_ref[...])
         + gt_ref[:, d_model:2 * d_model] * _dot(o_rwkv, wor_ref[...])
         + gt_ref[:, 2 * d_model:3 * d_model] * _dot(os_ref[...], wos_ref[...]))
    mix = _dot(m, wout_ref[...])
    o_ref[...] = x_ref[...] + mod_ref[2:3, :] * mix


def _merge(xs, modt, o_attn, y, aux, o_sgu, gates, ln_w, ln_b, gmean, woa, wor, wos, wout, n_ctx_tiles):
    b, s, d = xs.shape
    tm = TOKEN_TILE
    tok = lambda width: pl.BlockSpec((None, tm, width), lambda bi, i: (bi, i, 0))
    tok2 = lambda width: pl.BlockSpec((2, None, tm, width), lambda bi, i: (0, bi, i, 0))
    full = lambda a: pl.BlockSpec(a.shape, lambda bi, i: (0,) * a.ndim)
    return pl.pallas_call(
        _merge_kernel,
        grid=(b, s // tm),
        in_specs=[
            tok(d),
            pl.BlockSpec((None, None, 6, d), lambda bi, i: (bi, jnp.where(i < n_ctx_tiles, 0, 1), 0, 0)),
            tok(ATT_Q_DIM), tok2(RWKV_DIM), tok2(2 * RWKV_DIM), tok(SGU_DIM), tok(N_BRANCH * d),
            full(ln_w), full(ln_b), full(gmean), full(woa), full(wor), full(wos), full(wout),
        ],
        out_specs=tok(d),
        out_shape=jax.ShapeDtypeStruct((b, s, d), F32),
        compiler_params=_cparams(("parallel", "parallel")),
        name="merge_out",
    )(xs, modt, o_attn, y, aux, o_sgu, gates, ln_w, ln_b, gmean, woa, wor, wos, wout)


def _ffn_kernel(x_ref, mod_ref, gain_ref, w1_ref, w3_ref, w2_ref, o_ref):
    x = x_ref[...]
    mod = mod_ref[...]
    hb = _rms_mod(x, gain_ref[...], mod[3:4], mod[4:5]).astype(BF16)
    a = jnp.dot(hb, w1_ref[...], preferred_element_type=F32)
    bq = jnp.dot(hb, w3_ref[...], preferred_element_type=F32)
    hid = (a * jax.nn.sigmoid(a)) * bq
    o_ref[...] = x + mod[5:6] * _dot(hid, w2_ref[...])


def _ffn(xs, modt, gain, w1, w3, w2, n_ctx_tiles, first_tile):
    b, s, d = xs.shape
    tm = TOKEN_TILE
    nt = s // tm - first_tile
    full = lambda a: pl.BlockSpec(a.shape, lambda bi, i: (0,) * a.ndim)
    return pl.pallas_call(
        _ffn_kernel,
        grid=(b, nt),
        in_specs=[
            pl.BlockSpec((None, tm, d), lambda bi, i: (bi, i + first_tile, 0)),
            pl.BlockSpec((None, None, 6, d), lambda bi, i: (bi, jnp.where(i + first_tile < n_ctx_tiles, 0, 1), 0, 0)),
            full(gain), full(w1), full(w3), full(w2),
        ],
        out_specs=pl.BlockSpec((None, tm, d), lambda bi, i: (bi, i, 0)),
        out_shape=jax.ShapeDtypeStruct((b, nt * tm, d), F32),
        compiler_params=_cparams(("parallel", "parallel")),
        name="swiglu_ffn",
    )(xs, modt, gain, w1, w3, w2)


def _rope_tables(n_ctx, n_lat):
    rows = n_lat // GRID_W
    row = jnp.repeat(jnp.arange(rows), GRID_W).astype(F32)
    col = jnp.tile(jnp.arange(GRID_W), rows).astype(F32)
    inv = ROPE_BASE ** (-jnp.arange(ROPE_FREQS, dtype=F32) / ROPE_FREQS)
    ang = jnp.concatenate([row[:, None] * inv, col[:, None] * inv], -1)
    cos, sin = jnp.cos(ang), jnp.sin(ang)
    cos = jnp.concatenate([jnp.ones((n_ctx, HEAD_DIM // 2), F32), cos], 0)
    sin = jnp.concatenate([jnp.zeros((n_ctx, HEAD_DIM // 2), F32), sin], 0)
    return jnp.tile(jnp.concatenate([cos, cos], -1), (1, 2)), jnp.tile(jnp.concatenate([-sin, sin], -1), (1, 2))


def _block_diag_ones(n, blk, scale):
    i = np.arange(n)
    return jnp.asarray(((i[:, None] // blk) == (i[None, :] // blk)).astype(np.float32) * scale, dtype=BF16)


def _cumsum_matrices(tm, lc):
    i = np.arange(tm)
    same = (i[:, None] // lc) == (i[None, :] // lc)
    fwd = same & (i[None, :] <= i[:, None])
    bwd = same & (i[None, :] >= i[:, None])
    mats = np.stack([np.concatenate([fwd, same], 0), np.concatenate([bwd, same], 0)]).astype(np.float32)
    return jnp.asarray(mats, dtype=BF16)


def kernel(x, c, ctx, c_ctx, w_mod, b_mod, norm_mix, norm_ffn, w_in, q_gain, k_gain, attn_sink,
           rwkv_conv, rwkv_w0, rwkv_w_up, rwkv_a0, rwkv_a_up, rwkv_k_k, rwkv_k_a, rwkv_r_k, rwkv_g_up,
           rwkv_ln_w, rwkv_ln_b, rwkv_vres_down, rwkv_vres_up, rwkv_vres_b,
           sgu_ln_w, sgu_ln_b, sgu_w, sgu_b, w_o_attn, w_o_rwkv, w_o_sgu, w_out,
           ffn_w1, ffn_w3, ffn_w2):
    b, n_lat, d = x.shape
    n_ctx = ctx.shape[1]
    depth = w_in.shape[0]
    tm = TOKEN_TILE
    assert n_ctx % tm == 0 and n_lat % tm == 0 and n_lat % GRID_W == 0
    n_ctx_tiles = n_ctx // tm
    cdim = RWKV_DIM
    rwkv_passes = 3

    xs = jnp.concatenate([ctx, x], axis=1)
    cos2, sin2 = _rope_tables(n_ctx, n_lat)
    gmean = _block_diag_ones(cdim, HEAD_DIM, 1.0 / HEAD_DIM)
    gmean2 = gmean[:LANES, :LANES]
    gsum = _block_diag_ones(cdim, RWKV_HEAD, 1.0)
    tri = _cumsum_matrices(tm, RWKV_CHUNK)
    bdm = _block_diag_ones(RWKV_PACK * RWKV_HEAD, RWKV_HEAD, 1.0)

    rows = -(-(b + 1) // 8) * 8
    cc = jnp.zeros((rows, d), F32).at[:b].set(c).at[b].set(c_ctx)
    mods = _modulation(cc, w_mod, b_mod)

    row2 = lambda a: a.reshape(1, -1)
    v_first = None
    for l in range(depth):
        last = l == depth - 1
        has_vres = l > 0
        mod_lat = mods[l, :b].reshape(b, 1, 6, d)
        mod_ctx = jnp.broadcast_to(mods[l, b].reshape(1, 1, 6, d), (b, 1, 6, d))
        modt = jnp.concatenate([mod_ctx, mod_lat], axis=1)

        w_ext = w_in[l]
        if has_vres:
            down = jnp.pad(rwkv_vres_down[l - 1], ((0, 0), (0, LANES - VRES_LORA)))
            w_ext = jnp.concatenate([w_ext, down], axis=1)
        w_ext = w_ext.astype(BF16)
        qg2 = jnp.tile(q_gain[l], 2).reshape(1, LANES)
        kg2 = jnp.tile(k_gain[l], 2).reshape(1, LANES)
        outs = _proj_in(xs, modt, row2(norm_mix[l]), w_ext, cos2, sin2, qg2, kg2, gmean2, n_ctx_tiles, has_vres)
        q, kv, rw, sg, gates = outs[:5]

        o_attn = _attention(q, kv, attn_sink[l], n_ctx)

        zeros_lora = jnp.zeros((2, DECAY_LORA, cdim), F32)
        wa = jnp.concatenate([jnp.concatenate([rwkv_w_up[l], zeros_lora], axis=2),
                              jnp.concatenate([zeros_lora, rwkv_a_up[l]], axis=2)], axis=1).astype(BF16)
        w0a0 = jnp.concatenate([rwkv_w0[l], rwkv_a0[l]], axis=1).reshape(2, 1, 2 * cdim)
        vres = None
        if has_vres:
            vup = jnp.pad(rwkv_vres_up[l - 1], ((0, LANES - VRES_LORA), (0, 0))).astype(BF16)
            vres = (v_first, outs[5], vup, row2(rwkv_vres_b[l - 1]))
        r_outs = _rwkv(rw, rwkv_conv[l], wa, w0a0, row2(rwkv_k_k[l]), row2(rwkv_k_a[l]), row2(rwkv_r_k[l]),
                       rwkv_g_up[l].astype(BF16), gsum, tri, bdm, n_ctx_tiles, rwkv_passes, emit_v=(l == 0), vres=vres)
        y, aux = r_outs[0], r_outs[1]
        if l == 0:
            v_first = r_outs[2][0]

        sgu_bb = jnp.broadcast_to(sgu_b[l][:, :, None], (SGU_GROUPS, SGU_CHUNK, SGU_DIM // SGU_GROUPS))
        o_sgu = _sgu(sg, row2(sgu_ln_w[l]), row2(sgu_ln_b[l]), sgu_w[l].astype(BF16), sgu_bb)

        xs = _merge(xs, modt, o_attn, y, aux, o_sgu, gates, row2(rwkv_ln_w[l]), row2(rwkv_ln_b[l]), gmean,
                    w_o_attn[l].astype(BF16), w_o_rwkv[l].astype(BF16), w_o_sgu[l].astype(BF16),
                    w_out[l].astype(BF16), n_ctx_tiles)
        xs = _ffn(xs, modt, row2(norm_ffn[l]), ffn_w1[l].astype(BF16), ffn_w3[l].astype(BF16),
                  ffn_w2[l].astype(BF16), n_ctx_tiles, first_tile=n_ctx_tiles if last else 0)
    return xs
```

```python
import functools
import math

import jax
import jax.numpy as jnp
import numpy as np
from jax import lax
from jax.experimental import pallas as pl
from jax.experimental.pallas import tpu as pltpu

HEAD_DIM = 64
ATT_HEADS = 8
ATT_KV_HEADS = 2
GQA_GROUP = ATT_HEADS // ATT_KV_HEADS
ATT_Q_DIM = ATT_HEADS * HEAD_DIM
ATT_KV_DIM = ATT_KV_HEADS * HEAD_DIM
ATT_BLOCK = 128
ATT_SCALE = HEAD_DIM ** -0.5
ROPE_BASE = 10000.0
ROPE_FREQS = HEAD_DIM // 4
GRID_W = 64
RWKV_HEADS = 8
RWKV_HEAD = 64
RWKV_DIM = RWKV_HEADS * RWKV_HEAD
DECAY_LORA = 64
ICLR_LORA = 64
VRES_LORA = 32
GATE_LORA = 128
RWKV_IN = 3 * RWKV_DIM + DECAY_LORA + ICLR_LORA + GATE_LORA
GN_EPS = 64e-5
SGU_CHUNK = 128
SGU_GROUPS = 4
SGU_DIM = 512
N_BRANCH = 3
EPS = 1e-6

LANES = 128
TOKEN_TILE = 256
RWKV_CHUNK = 64
MXU_DIM = 256
RWKV_PACK = MXU_DIM // RWKV_HEAD
VMEM_LIMIT = 56 * 1024 * 1024

BF16 = jnp.bfloat16
F32 = jnp.float32


def _cparams(sem):
    return pltpu.CompilerParams(dimension_semantics=sem, vmem_limit_bytes=VMEM_LIMIT)


def _dot(a, b):
    return jnp.dot(a.astype(BF16), b.astype(BF16), preferred_element_type=F32)


def _split2(x):
    hi = x.astype(BF16)
    lo = (x - hi.astype(F32)).astype(BF16)
    return hi, lo


def _dot_hi_rhs_exact(x, m_bf16):
    hi, lo = _split2(x)
    return (jnp.dot(hi, m_bf16, preferred_element_type=F32)
            + jnp.dot(lo, m_bf16, preferred_element_type=F32))


def _rms_mod(x, gain, shift, scale):
    ms = jnp.mean(x * x, axis=-1, keepdims=True)
    return (x * lax.rsqrt(ms + EPS) * gain) * (1.0 + scale) + shift


def _mod_kernel(c_ref, w_ref, b_ref, o_ref):
    c = c_ref[...]
    o_ref[...] = _dot(c * jax.nn.sigmoid(c), w_ref[...]) + b_ref[...]


def _modulation(cc, w_mod, b_mod):
    nl, d, n6 = w_mod.shape
    rows = cc.shape[0]
    tn = 1536
    return pl.pallas_call(
        _mod_kernel,
        grid=(nl, n6 // tn),
        in_specs=[
            pl.BlockSpec((rows, d), lambda l, j: (0, 0)),
            pl.BlockSpec((None, d, tn), lambda l, j: (l, 0, j)),
            pl.BlockSpec((None, 1, tn), lambda l, j: (l, 0, j)),
        ],
        out_specs=pl.BlockSpec((None, rows, tn), lambda l, j: (l, 0, j)),
        out_shape=jax.ShapeDtypeStruct((nl, rows, n6), F32),
        compiler_params=_cparams(("parallel", "parallel")),
        name="modulation",
    )(cc, w_mod, b_mod.reshape(nl, 1, n6))


def _proj_in_kernel(has_vres, x_ref, mod_ref, gain_ref, w_ref, cos_ref, sin_ref, qg_ref, kg_ref, gm_ref,
                    lnw_ref, lnb_ref, sw_ref, sb_ref, q_ref, kv_ref, rw_ref, os_ref, *maybe_hd_ref):
    x = x_ref[...]
    mod = mod_ref[...]
    hb = _rms_mod(x, gain_ref[...], mod[0:1], mod[1:2]).astype(BF16)
    cos = cos_ref[...]
    sin = sin_ref[...]
    lane = lax.broadcasted_iota(jnp.int32, cos.shape, 1)
    first_half = (lane % HEAD_DIM) < (HEAD_DIM // 2)

    def norm_rope(p, gain):
        ms = _dot_hi_rhs_exact(p * p, gm_ref[...])
        t = p * lax.rsqrt(ms + EPS) * gain
        swapped = jnp.where(first_half, pltpu.roll(t, LANES - HEAD_DIM // 2, 1), pltpu.roll(t, HEAD_DIM // 2, 1))
        return t * cos + swapped * sin

    col = 0
    for j in range(ATT_Q_DIM // LANES):
        p = jnp.dot(hb, w_ref[:, col:col + LANES], preferred_element_type=F32)
        q_ref[:, j * LANES:(j + 1) * LANES] = norm_rope(p, qg_ref[...]).astype(BF16)
        col += LANES
    p = jnp.dot(hb, w_ref[:, col:col + LANES], preferred_element_type=F32)
    kv_ref[:, 0:LANES] = norm_rope(p, kg_ref[...]).astype(BF16)
    col += LANES
    kv_ref[:, LANES:2 * LANES] = jnp.dot(hb, w_ref[:, col:col + LANES], preferred_element_type=F32).astype(BF16)
    col += LANES
    step = 256
    for j in range(0, RWKV_IN, step):
        rw_ref[:, j:j + step] = jnp.dot(hb, w_ref[:, col + j:col + j + step], preferred_element_type=F32)
    col += RWKV_IN

    u = jax.nn.gelu(jnp.dot(hb, w_ref[:, col:col + SGU_DIM], preferred_element_type=F32))
    vg = jax.nn.gelu(jnp.dot(hb, w_ref[:, col + SGU_DIM:col + 2 * SGU_DIM], preferred_element_type=F32))
    col += 2 * SGU_DIM
    mu = jnp.mean(vg, axis=-1, keepdims=True)
    var = jnp.mean(jnp.square(vg - mu), axis=-1, keepdims=True)
    vn = ((vg - mu) * lax.rsqrt(var + EPS) * lnw_ref[...] + lnb_ref[...]).astype(BF16)
    gd = SGU_DIM // SGU_GROUPS
    for ck in range(x.shape[0] // SGU_CHUNK):
        rows = slice(ck * SGU_CHUNK, (ck + 1) * SGU_CHUNK)
        for g in range(SGU_GROUPS):
            cols = slice(g * gd, (g + 1) * gd)
            sp = jnp.dot(sw_ref[g], vn[rows, cols], preferred_element_type=F32) + sb_ref[g]
            os_ref[rows, cols] = (u[rows, cols] * sp).astype(BF16)
    if has_vres:
        maybe_hd_ref[0][...] = jnp.dot(hb, w_ref[:, col:col + LANES], preferred_element_type=F32)


def _proj_in(xs, modt, gain, w_ext, cos2, sin2, qg, kg, gmean, ln_w, ln_b, sgu_w, sgu_bb, n_ctx_tiles, has_vres):
    b, s, d = xs.shape
    tm = TOKEN_TILE
    nt = s // tm
    tok = lambda width: pl.BlockSpec((None, tm, width), lambda bi, i: (bi, i, 0))
    full = lambda a: pl.BlockSpec(a.shape, lambda bi, i: (0,) * a.ndim)
    out_shapes = [
        jax.ShapeDtypeStruct((b, s, ATT_Q_DIM), BF16),
        jax.ShapeDtypeStruct((b, s, 2 * ATT_KV_DIM), BF16),
        jax.ShapeDtypeStruct((b, s, RWKV_IN), F32),
        jax.ShapeDtypeStruct((b, s, SGU_DIM), BF16),
    ]
    out_specs = [tok(ATT_Q_DIM), tok(2 * ATT_KV_DIM), tok(RWKV_IN), tok(SGU_DIM)]
    if has_vres:
        out_shapes.append(jax.ShapeDtypeStruct((b, s, LANES), F32))
        out_specs.append(tok(LANES))
    return pl.pallas_call(
        functools.partial(_proj_in_kernel, has_vres),
        grid=(b, nt),
        in_specs=[
            tok(d),
            pl.BlockSpec((None, None, 6, d), lambda bi, i: (bi, jnp.where(i < n_ctx_tiles, 0, 1), 0, 0)),
            full(gain), full(w_ext),
            pl.BlockSpec((tm, LANES), lambda bi, i: (i, 0)),
            pl.BlockSpec((tm, LANES), lambda bi, i: (i, 0)),
            full(qg), full(kg), full(gmean), full(ln_w), full(ln_b), full(sgu_w), full(sgu_bb),
        ],
        out_specs=out_specs,
        out_shape=out_shapes,
        compiler_params=_cparams(("parallel", "parallel")),
        name="proj_in",
    )(xs, modt, gain, w_ext, cos2, sin2, qg, kg, gmean, ln_w, ln_b, sgu_w, sgu_bb)


def _attn_kernel(n_ctx_blocks, n_blocks, sink_ref, q_ref, kvc_ref, kvp_ref, kvo_ref, kvn_ref, o_ref):
    j = pl.program_id(1)
    blk = ATT_BLOCK
    is_lat = j >= n_ctx_blocks
    ok_prev = j >= n_ctx_blocks + 1
    ok_next = jnp.logical_and(is_lat, j <= n_blocks - 2)
    rows = GQA_GROUP * blk
    nband = 3 * blk
    qq = lax.broadcasted_iota(jnp.int32, (rows, nband), 0) % blk
    kcol = lax.broadcasted_iota(jnp.int32, (rows, nband), 1)
    kk = kcol % blk
    off_prev = jnp.where(ok_prev, 0, blk)
    off_next = jnp.where(ok_next, 0, blk)
    own_end = jnp.where(is_lat, 2 * blk, 0)
    band_mask = (((kcol < blk) & (kk >= qq + off_prev))
                 | ((kcol >= blk) & (kcol < own_end))
                 | ((kcol >= 2 * blk) & (kk <= qq - off_next)))
    rgrp = lax.broadcasted_iota(jnp.int32, (rows, 1), 0) // blk
    q = q_ref[...] * ATT_SCALE
    kvc, kvp, kvo, kvn = kvc_ref[...], kvp_ref[...], kvo_ref[...], kvn_ref[...]
    nt_dims = (((1,), (1,)), ((), ()))
    outs = []
    for kh in range(ATT_KV_HEADS):
        h0 = kh * GQA_GROUP
        qh = jnp.concatenate([q[:, (h0 + g) * HEAD_DIM:(h0 + g + 1) * HEAD_DIM] for g in range(GQA_GROUP)], axis=0)
        ksl = slice(kh * HEAD_DIM, (kh + 1) * HEAD_DIM)
        vsl = slice(ATT_KV_DIM + kh * HEAD_DIM, ATT_KV_DIM + (kh + 1) * HEAD_DIM)
        k_band = jnp.concatenate([kvp[:, ksl], kvo[:, ksl], kvn[:, ksl]], axis=0)
        v_band = jnp.concatenate([kvp[:, vsl], kvo[:, vsl], kvn[:, vsl]], axis=0)
        s_ctx = lax.dot_general(qh, kvc[:, ksl], nt_dims, preferred_element_type=F32)
        s_band = jnp.where(band_mask, lax.dot_general(qh, k_band, nt_dims, preferred_element_type=F32), -jnp.inf)
        sink = jnp.zeros((rows, 1), F32)
        for g in range(GQA_GROUP):
            sink = jnp.where(rgrp == g, sink_ref[h0 + g], sink)
        m = jnp.maximum(jnp.maximum(jnp.max(s_ctx, axis=-1, keepdims=True), jnp.max(s_band, axis=-1, keepdims=True)), sink)
        p_ctx = jnp.exp(s_ctx - m)
        p_band = jnp.exp(s_band - m)
        denom = (jnp.sum(p_ctx, axis=-1, keepdims=True) + jnp.sum(p_band, axis=-1, keepdims=True)) + jnp.exp(sink - m)
        o = (jnp.dot(p_ctx.astype(BF16), kvc[:, vsl], preferred_element_type=F32)
             + jnp.dot(p_band.astype(BF16), v_band, preferred_element_type=F32)) / denom
        outs.extend(o[g * blk:(g + 1) * blk] for g in range(GQA_GROUP))
    o_ref[...] = jnp.concatenate(outs, axis=1).astype(BF16)


def _attention(q, kv, sink, n_ctx):
    b, s, _ = q.shape
    blk = ATT_BLOCK
    nb = s // blk
    ncb = n_ctx // blk
    kvw = kv.shape[-1]
    return pl.pallas_call(
        functools.partial(_attn_kernel, ncb, nb),
        grid=(b, nb),
        in_specs=[
            pl.BlockSpec(memory_space=pltpu.SMEM),
            pl.BlockSpec((None, blk, ATT_Q_DIM), lambda bi, j: (bi, j, 0)),
            pl.BlockSpec((None, n_ctx, kvw), lambda bi, j: (bi, 0, 0)),
            pl.BlockSpec((None, blk, kvw), lambda bi, j: (bi, jnp.maximum(j - 1, 0), 0)),
            pl.BlockSpec((None, blk, kvw), lambda bi, j: (bi, j, 0)),
            pl.BlockSpec((None, blk, kvw), lambda bi, j: (bi, jnp.minimum(j + 1, nb - 1), 0)),
        ],
        out_specs=pl.BlockSpec((None, blk, ATT_Q_DIM), lambda bi, j: (bi, j, 0)),
        out_shape=jax.ShapeDtypeStruct((b, s, ATT_Q_DIM), BF16),
        compiler_params=_cparams(("parallel", "parallel")),
        name="window_attention",
    )(sink, q, kv, kv, kv, kv)


def _rwkv_tile(d, i, n_tiles):
    return jnp.where(d == 0, i, jnp.where(i == 0, 0, n_tiles - i))


def _rwkv_kernel(has_vres, emit_v, n_ctx_tiles, n_tiles, *refs):
    it = iter(refs)
    rw_ref, rwp_ref, rwn_ref = next(it), next(it), next(it)
    vf_ref = hd_ref = vup_ref = vb_ref = None
    if has_vres:
        vf_ref, hd_ref = next(it), next(it)
    conv_ref, wa_ref, w0a0_ref, kkw_ref, kaw_ref, rkw_ref, gup_ref = (next(it) for _ in range(7))
    if has_vres:
        vup_ref, vb_ref = next(it), next(it)
    gsum_ref, tri_ref, bdm_ref = next(it), next(it), next(it)
    y_ref, aux_ref = next(it), next(it)
    v_out_ref = next(it) if emit_v else None
    st_ref, fa_ref, fr_ref, fb_ref, fk_ref, fbh_ref, fkh_ref, fv_ref, rm_ref, yc_ref = (next(it) for _ in range(10))

    d = pl.program_id(1)
    i = pl.program_id(2)
    t = _rwkv_tile(d, i, n_tiles)
    tm = rw_ref.shape[0]
    c_dim = RWKV_DIM
    lc = RWKV_CHUNK
    n_chunks = tm // lc
    gw = RWKV_PACK * RWKV_HEAD
    n_groups = c_dim // gw

    @pl.when(i == 0)
    def _():
        st_ref[...] = jnp.zeros_like(st_ref)

    x = rw_ref[...]
    ok_prev = jnp.logical_and(t != 0, t != n_ctx_tiles)
    ok_next = jnp.logical_and(t != n_ctx_tiles - 1, t != n_tiles - 1)
    halo_p = jnp.where(ok_prev, rwp_ref[7:8, :], 0.0)
    halo_n = jnp.where(ok_next, rwn_ref[0:1, :], 0.0)
    row = lax.broadcasted_iota(jnp.int32, x.shape, 0)
    xp = jnp.where(row == 0, halo_p, pltpu.roll(x, 1, 0))
    xn = jnp.where(row == tm - 1, halo_n, pltpu.roll(x, tm - 1, 0))
    cw = conv_ref[...]
    cv = (xp * cw[0:1] + x * cw[1:2]) + xn * cw[2:3]
    r = cv[:, 0:c_dim]
    k = cv[:, c_dim:2 * c_dim]
    v = cv[:, 2 * c_dim:3 * c_dim]
    xwa = cv[:, 3 * c_dim:3 * c_dim + LANES]
    xg = cv[:, 3 * c_dim + LANES:3 * c_dim + 2 * LANES]

    if emit_v:
        v_out_ref[...] = v
    if has_vres:
        gate = jax.nn.sigmoid(vb_ref[...] + _dot(hd_ref[...], vup_ref[...]))
        v = v + (vf_ref[...] - v) * gate

    gsum = gsum_ref[...]
    kk = k * kkw_ref[...]
    kk = kk * lax.rsqrt(_dot_hi_rhs_exact(kk * kk, gsum) + 1e-12)
    lane = lax.broadcasted_iota(jnp.int32, xwa.shape, 1)
    lora_in = jnp.where(lane < DECAY_LORA, jnp.tanh(xwa), xwa)
    pre = _dot(lora_in, wa_ref[...]) + w0a0_ref[...]
    w_pre = pre[:, 0:c_dim]
    a_sig = jax.nn.sigmoid(pre[:, c_dim:2 * c_dim])
    lw = -jnp.exp(-jax.nn.softplus(-w_pre) - 0.5)
    kd = k * (1.0 + (a_sig - 1.0) * kaw_ref[...])
    bvec = kk * a_sig
    bonus = _dot_hi_rhs_exact(r * kd * rkw_ref[...], gsum) * v
    g = _dot(jax.nn.sigmoid(xg), gup_ref[...])
    aux_ref[:, 0:c_dim] = bonus
    aux_ref[:, c_dim:2 * c_dim] = g

    hi, lo = _split2(lw)
    tri = tri_ref[...]
    cs = jnp.dot(tri, hi, preferred_element_type=F32) + jnp.dot(tri, lo, preferred_element_type=F32)
    c_inc = cs[0:tm]
    c_tot = cs[tm:2 * tm]
    e_inc = jnp.exp(c_inc)
    e_neg = jnp.exp(-c_inc)
    e_rem = jnp.exp(c_tot - c_inc)
    fa_ref[...] = -kk * jnp.exp(c_inc - lw)
    fr_ref[...] = r * e_inc
    fb_ref[...] = bvec * e_neg
    fk_ref[...] = kd * e_neg
    fbh_ref[...] = bvec * e_rem
    fkh_ref[...] = kd * e_rem
    fv_ref[...] = v

    bdm = bdm_ref[...]

    def bd(xb):
        return jnp.concatenate([xb] * RWKV_PACK, axis=0) * bdm

    low_half = lax.broadcasted_iota(jnp.int32, (lc, 2 * lc), 1) < lc

    def head_transpose(xf):
        w = jnp.concatenate([xf, xf], axis=0).T
        pairs = [jnp.where(low_half, w[(2 * j) * lc:(2 * j + 1) * lc], w[(2 * j + 1) * lc:(2 * j + 2) * lc])
                 for j in range(RWKV_PACK // 2)]
        return jnp.concatenate(pairs, axis=1)

    mm = lambda p, q: jnp.dot(p, q, preferred_element_type=F32)
    ri = lax.broadcasted_iota(jnp.int32, (2 * lc, 2 * gw), 0)
    ci = lax.broadcasted_iota(jnp.int32, (2 * lc, 2 * gw), 1) % lc
    fwd = d == 0
    sgn = jnp.where(fwd, 1, -1)
    amask = jnp.logical_or(jnp.logical_and(ri < lc, sgn * (ri - ci) > 0),
                           jnp.logical_and(ri >= lc, sgn * (ri - lc - ci) >= 0))
    eye_p = (lax.broadcasted_iota(jnp.int32, (lc, gw), 0)
             == lax.broadcasted_iota(jnp.int32, (lc, gw), 1) % lc).astype(F32)
    n_levels = int(math.log2(lc))

    probs = [(c, g) for c in range(n_chunks) for g in range(n_groups)]
    blk = lambda ref, c, g: ref[c * lc:(c + 1) * lc, g * gw:(g + 1) * gw]
    ab = [blk(fa_ref, c, g).astype(BF16) for c, g in probs]
    vb = [blk(fv_ref, c, g).astype(BF16) for c, g in probs]
    a_all = []
    for j, (c, g) in enumerate(probs):
        lhs = jnp.concatenate([ab[j], blk(fr_ref, c, g).astype(BF16)], axis=0)
        rhs_t = jnp.concatenate([bd(blk(fb_ref, c, g).astype(BF16)), bd(blk(fk_ref, c, g).astype(BF16))], axis=0)
        full = lax.dot_general(lhs, rhs_t, (((1,), (1,)), ((), ())), preferred_element_type=F32)
        a_all.append(jnp.where(amask, full, 0.0))
    a_rb = [m[lc:2 * lc, 0:gw].astype(BF16) for m in a_all]
    a_rk = [m[lc:2 * lc, gw:2 * gw].astype(BF16) for m in a_all]
    t_p = [eye_p + m[0:lc, 0:gw] for m in a_all]
    pb = [m[0:lc, 0:gw].astype(BF16) for m in a_all]
    pb = [mm(p, bd(p)).astype(BF16) for p in pb]
    for lvl in range(1, n_levels):
        if lvl < n_levels - 1:
            res = [mm(jnp.concatenate([p, t.astype(BF16)], axis=0), bd(p)) for p, t in zip(pb, t_p)]
            pb = [m[0:lc].astype(BF16) for m in res]
            t_p = [t + m[lc:2 * lc] for t, m in zip(t_p, res)]
        else:
            t_p = [t + mm(t.astype(BF16), bd(p)) for p, t in zip(pb, t_p)]
    u = [mm(m[0:lc, gw:2 * gw].astype(BF16), bd(v)) for m, v in zip(a_all, vb)]
    x = [mm(t.astype(BF16), jnp.concatenate([bd(a), bd(uu.astype(BF16))], axis=1))
         for t, a, uu in zip(t_p, ab, u)]
    a1b = [m[:, 0:gw].astype(BF16) for m in x]
    u1b = [m[:, gw:2 * gw].astype(BF16) for m in x]
    for j, (c, g) in enumerate(probs):
        el_c = jnp.exp(c_tot[c * lc:c * lc + 1, g * gw:(g + 1) * gw])
        zb = head_transpose(blk(fbh_ref, c, g)).astype(BF16)
        zk = head_transpose(blk(fkh_ref, c, g)).astype(BF16)
        res = mm(jnp.concatenate([a_rb[j], zb], axis=0), bd(a1b[j]))
        rm_ref[c, g] = jnp.concatenate([blk(fr_ref, c, g) + res[0:lc], res[lc:2 * lc] + eye_p * el_c], axis=0)
        yc_ref[c, g] = mm(jnp.concatenate([jnp.concatenate([a_rb[j], a_rk[j]], axis=1),
                                           jnp.concatenate([zb, zk], axis=1)], axis=0),
                          jnp.concatenate([bd(u1b[j]), bd(vb[j])], axis=0))

    for n in range(n_chunks):
        cc = jnp.where(fwd, n, n_chunks - 1 - n)
        ys = []
        for g in range(n_groups):
            tot = mm(rm_ref[cc, g].astype(BF16), bd(st_ref[g].astype(BF16))) + yc_ref[cc, g]
            ys.append(tot[0:lc])
            st_ref[g] = tot[lc:2 * lc]
        y_ref[pl.ds(pl.multiple_of(cc * lc, lc), lc), :] = jnp.concatenate(ys, axis=1)


def _rwkv(rw, conv_w, wa, w0a0, k_k, k_a, r_k, g_up, gsum, tri, bdm, n_ctx_tiles, emit_v, vres=None):
    b, s, _ = rw.shape
    tm = TOKEN_TILE
    nt = s // tm
    c = RWKV_DIM
    halo_blocks = tm // 8
    last_halo = s // 8 - 1
    tile = lambda d, i: _rwkv_tile(d, i, nt)
    tok = lambda width: pl.BlockSpec((None, tm, width), lambda bi, d, i: (bi, tile(d, i), 0))
    full = lambda a: pl.BlockSpec(a.shape, lambda bi, d, i: (0,) * a.ndim)
    per_dir = lambda a: pl.BlockSpec((None,) + a.shape[1:], lambda bi, d, i: (d,) + (0,) * (a.ndim - 1))
    in_specs = [
        tok(RWKV_IN),
        pl.BlockSpec((None, 8, RWKV_IN), lambda bi, d, i: (bi, jnp.maximum(tile(d, i) * halo_blocks - 1, 0), 0)),
        pl.BlockSpec((None, 8, RWKV_IN), lambda bi, d, i: (bi, jnp.minimum((tile(d, i) + 1) * halo_blocks, last_halo), 0)),
    ]
    args = [rw, rw, rw]
    if vres is not None:
        v_first, hd, vup, vb = vres
        in_specs += [tok(c), tok(LANES)]
        args += [v_first, hd]
    in_specs += [full(conv_w), per_dir(wa), per_dir(w0a0), full(k_k), full(k_a), full(r_k), full(g_up)]
    args += [conv_w, wa, w0a0, k_k, k_a, r_k, g_up]
    if vres is not None:
        in_specs += [full(vup), full(vb)]
        args += [vup, vb]
    in_specs += [full(gsum), per_dir(tri), full(bdm)]
    args += [gsum, tri, bdm]
    out_tok = lambda width: pl.BlockSpec((None, None, tm, width), lambda bi, d, i: (d, bi, tile(d, i), 0))
    out_specs = [out_tok(c), out_tok(2 * c)]
    out_shape = [jax.ShapeDtypeStruct((2, b, s, c), F32), jax.ShapeDtypeStruct((2, b, s, 2 * c), F32)]
    if emit_v:
        out_specs.append(out_tok(c))
        out_shape.append(jax.ShapeDtypeStruct((2, b, s, c), F32))
    lc = RWKV_CHUNK
    gw = RWKV_PACK * RWKV_HEAD
    n_groups, n_chunks = c // gw, tm // lc
    feat_buf = pltpu.VMEM((tm, c), F32)
    scratch = ([pltpu.VMEM((n_groups, lc, gw), F32)] + [feat_buf] * 7
               + [pltpu.VMEM((n_chunks, n_groups, 2 * lc, gw), F32)] * 2)
    return pl.pallas_call(
        functools.partial(_rwkv_kernel, vres is not None, emit_v, n_ctx_tiles, nt),
        grid=(b, 2, nt),
        in_specs=in_specs,
        out_specs=out_specs,
        out_shape=out_shape,
        scratch_shapes=scratch,
        compiler_params=_cparams(("parallel", "parallel", "arbitrary")),
        name="rwkv7_scan",
    )(*args)


def _merge_kernel(x_ref, mod_ref, gain_ref, wgt_ref, oa_ref, y_ref, aux_ref, os_ref, lnw_ref, lnb_ref, gm_ref,
                  woa_ref, wor_ref, wos_ref, wout_ref, o_ref):
    d_model = x_ref.shape[-1]
    c = RWKV_DIM
    x = x_ref[...]
    mod = mod_ref[...]
    hb = _rms_mod(x, gain_ref[...], mod[0:1], mod[1:2]).astype(BF16)
    y = y_ref[0] + y_ref[1]
    gm = gm_ref[...]
    mu = _dot_hi_rhs_exact(y, gm)
    yc = y - mu
    var = _dot_hi_rhs_exact(yc * yc, gm)
    yn = yc * lax.rsqrt(var + GN_EPS) * lnw_ref[...] + lnb_ref[...]
    bonus = aux_ref[0, :, 0:c] + aux_ref[1, :, 0:c]
    o_rwkv = ((yn + bonus) * aux_ref[0, :, c:2 * c]).astype(BF16)
    gate = lambda j: jax.nn.sigmoid(jnp.dot(hb, wgt_ref[:, j * d_model:(j + 1) * d_model], preferred_element_type=F32))
    m = (gate(0) * jnp.dot(oa_ref[...], woa_ref[...], preferred_element_type=F32)
         + gate(1) * jnp.dot(o_rwkv, wor_ref[...], preferred_element_type=F32)
         + gate(2) * jnp.dot(os_ref[...], wos_ref[...], preferred_element_type=F32))
    mix = _dot(m, wout_ref[...])
    o_ref[...] = x + mod[2:3] * mix


def _merge(xs, modt, gain, w_gt, o_attn, y, aux, o_sgu, ln_w, ln_b, gmean, woa, wor, wos, wout, n_ctx_tiles):
    b, s, d = xs.shape
    tm = TOKEN_TILE
    tok = lambda width: pl.BlockSpec((None, tm, width), lambda bi, i: (bi, i, 0))
    tok2 = lambda width: pl.BlockSpec((2, None, tm, width), lambda bi, i: (0, bi, i, 0))
    full = lambda a: pl.BlockSpec(a.shape, lambda bi, i: (0,) * a.ndim)
    return pl.pallas_call(
        _merge_kernel,
        grid=(b, s // tm),
        in_specs=[
            tok(d),
            pl.BlockSpec((None, None, 6, d), lambda bi, i: (bi, jnp.where(i < n_ctx_tiles, 0, 1), 0, 0)),
            full(gain), full(w_gt),
            tok(ATT_Q_DIM), tok2(RWKV_DIM), tok2(2 * RWKV_DIM), tok(SGU_DIM),
            full(ln_w), full(ln_b), full(gmean), full(woa), full(wor), full(wos), full(wout),
        ],
        out_specs=tok(d),
        out_shape=jax.ShapeDtypeStruct((b, s, d), F32),
        compiler_params=_cparams(("parallel", "parallel")),
        name="merge_out",
    )(xs, modt, gain, w_gt, o_attn, y, aux, o_sgu, ln_w, ln_b, gmean, woa, wor, wos, wout)


def _ffn_kernel(x_ref, mod_ref, gain_ref, w1_ref, w3_ref, w2_ref, o_ref):
    x = x_ref[...]
    mod = mod_ref[...]
    hb = _rms_mod(x, gain_ref[...], mod[3:4], mod[4:5]).astype(BF16)
    a = jnp.dot(hb, w1_ref[...], preferred_element_type=F32)
    bq = jnp.dot(hb, w3_ref[...], preferred_element_type=F32)
    hid = (a * jax.nn.sigmoid(a)) * bq
    o_ref[...] = x + mod[5:6] * _dot(hid, w2_ref[...])


def _ffn(xs, modt, gain, w1, w3, w2, n_ctx_tiles, first_tile):
    b, s, d = xs.shape
    tm = TOKEN_TILE
    nt = s // tm - first_tile
    full = lambda a: pl.BlockSpec(a.shape, lambda bi, i: (0,) * a.ndim)
    return pl.pallas_call(
        _ffn_kernel,
        grid=(b, nt),
        in_specs=[
            pl.BlockSpec((None, tm, d), lambda bi, i: (bi, i + first_tile, 0)),
            pl.BlockSpec((None, None, 6, d), lambda bi, i: (bi, jnp.where(i + first_tile < n_ctx_tiles, 0, 1), 0, 0)),
            full(gain), full(w1), full(w3), full(w2),
        ],
        out_specs=pl.BlockSpec((None, tm, d), lambda bi, i: (bi, i, 0)),
        out_shape=jax.ShapeDtypeStruct((b, nt * tm, d), F32),
        compiler_params=_cparams(("parallel", "parallel")),
        name="swiglu_ffn",
    )(xs, modt, gain, w1, w3, w2)


def _rope_tables(n_ctx, n_lat):
    rows = n_lat // GRID_W
    row = jnp.repeat(jnp.arange(rows), GRID_W).astype(F32)
    col = jnp.tile(jnp.arange(GRID_W), rows).astype(F32)
    inv = ROPE_BASE ** (-jnp.arange(ROPE_FREQS, dtype=F32) / ROPE_FREQS)
    ang = jnp.concatenate([row[:, None] * inv, col[:, None] * inv], -1)
    cos, sin = jnp.cos(ang), jnp.sin(ang)
    cos = jnp.concatenate([jnp.ones((n_ctx, HEAD_DIM // 2), F32), cos], 0)
    sin = jnp.concatenate([jnp.zeros((n_ctx, HEAD_DIM // 2), F32), sin], 0)
    return jnp.tile(jnp.concatenate([cos, cos], -1), (1, 2)), jnp.tile(jnp.concatenate([-sin, sin], -1), (1, 2))


def _block_diag_ones(n, blk, scale):
    i = np.arange(n)
    return jnp.asarray(((i[:, None] // blk) == (i[None, :] // blk)).astype(np.float32) * scale, dtype=BF16)


def _cumsum_matrices(tm, lc):
    i = np.arange(tm)
    same = (i[:, None] // lc) == (i[None, :] // lc)
    fwd = same & (i[None, :] <= i[:, None])
    bwd = same & (i[None, :] >= i[:, None])
    mats = np.stack([np.concatenate([fwd, same], 0), np.concatenate([bwd, same], 0)]).astype(np.float32)
    return jnp.asarray(mats, dtype=BF16)


def kernel(x, c, ctx, c_ctx, w_mod, b_mod, norm_mix, norm_ffn, w_in, q_gain, k_gain, attn_sink,
           rwkv_conv, rwkv_w0, rwkv_w_up, rwkv_a0, rwkv_a_up, rwkv_k_k, rwkv_k_a, rwkv_r_k, rwkv_g_up,
           rwkv_ln_w, rwkv_ln_b, rwkv_vres_down, rwkv_vres_up, rwkv_vres_b,
           sgu_ln_w, sgu_ln_b, sgu_w, sgu_b, w_o_attn, w_o_rwkv, w_o_sgu, w_out,
           ffn_w1, ffn_w3, ffn_w2):
    b, n_lat, d = x.shape
    n_ctx = ctx.shape[1]
    depth = w_in.shape[0]
    tm = TOKEN_TILE
    assert n_ctx % tm == 0 and n_lat % tm == 0 and n_lat % GRID_W == 0 and 2 * RWKV_CHUNK == LANES
    n_ctx_tiles = n_ctx // tm
    cdim = RWKV_DIM

    xs = jnp.concatenate([ctx, x], axis=1)
    cos2, sin2 = _rope_tables(n_ctx, n_lat)
    gmean = _block_diag_ones(cdim, HEAD_DIM, 1.0 / HEAD_DIM)
    gmean2 = gmean[:LANES, :LANES]
    gsum = _block_diag_ones(cdim, RWKV_HEAD, 1.0)
    tri = _cumsum_matrices(tm, RWKV_CHUNK)
    bdm = _block_diag_ones(RWKV_PACK * RWKV_HEAD, RWKV_HEAD, 1.0)

    rows = -(-(b + 1) // 8) * 8
    cc = jnp.zeros((rows, d), F32).at[:b].set(c).at[b].set(c_ctx)
    mods = _modulation(cc, w_mod, b_mod)

    row2 = lambda a: a.reshape(1, -1)
    v_first = None
    for l in range(depth):
        last = l == depth - 1
        has_vres = l > 0
        mod_lat = mods[l, :b].reshape(b, 1, 6, d)
        mod_ctx = jnp.broadcast_to(mods[l, b].reshape(1, 1, 6, d), (b, 1, 6, d))
        modt = jnp.concatenate([mod_ctx, mod_lat], axis=1)

        n_proj = w_in.shape[2] - N_BRANCH * d
        w_ext = w_in[l, :, :n_proj]
        if has_vres:
            down = jnp.pad(rwkv_vres_down[l - 1], ((0, 0), (0, LANES - VRES_LORA)))
            w_ext = jnp.concatenate([w_ext, down], axis=1)
        w_ext = w_ext.astype(BF16)
        w_gt = w_in[l, :, n_proj:].astype(BF16)
        qg2 = jnp.tile(q_gain[l], 2).reshape(1, LANES)
        kg2 = jnp.tile(k_gain[l], 2).reshape(1, LANES)
        sgu_bb = jnp.broadcast_to(sgu_b[l][:, :, None], (SGU_GROUPS, SGU_CHUNK, SGU_DIM // SGU_GROUPS))
        outs = _proj_in(xs, modt, row2(norm_mix[l]), w_ext, cos2, sin2, qg2, kg2, gmean2,
                        row2(sgu_ln_w[l]), row2(sgu_ln_b[l]), sgu_w[l].astype(BF16), sgu_bb, n_ctx_tiles, has_vres)
        q, kv, rw, o_sgu = outs[:4]

        o_attn = _attention(q, kv, attn_sink[l], n_ctx)

        zeros_lora = jnp.zeros((2, DECAY_LORA, cdim), F32)
        wa = jnp.concatenate([jnp.concatenate([rwkv_w_up[l], zeros_lora], axis=2),
                              jnp.concatenate([zeros_lora, rwkv_a_up[l]], axis=2)], axis=1).astype(BF16)
        w0a0 = jnp.concatenate([rwkv_w0[l], rwkv_a0[l]], axis=1).reshape(2, 1, 2 * cdim)
        vres = None
        if has_vres:
            vup = jnp.pad(rwkv_vres_up[l - 1], ((0, LANES - VRES_LORA), (0, 0))).astype(BF16)
            vres = (v_first, outs[4], vup, row2(rwkv_vres_b[l - 1]))
        r_outs = _rwkv(rw, rwkv_conv[l], wa, w0a0, row2(rwkv_k_k[l]), row2(rwkv_k_a[l]), row2(rwkv_r_k[l]),
                       rwkv_g_up[l].astype(BF16), gsum, tri, bdm, n_ctx_tiles, emit_v=(l == 0), vres=vres)
        y, aux = r_outs[0], r_outs[1]
        if l == 0:
            v_first = r_outs[2][0]

        xs = _merge(xs, modt, row2(norm_mix[l]), w_gt, o_attn, y, aux, o_sgu, row2(rwkv_ln_w[l]), row2(rwkv_ln_b[l]),
                    gmean, w_o_attn[l].astype(BF16), w_o_rwkv[l].astype(BF16), w_o_sgu[l].astype(BF16),
                    w_out[l].astype(BF16), n_ctx_tiles)
        xs = _ffn(xs, modt, row2(norm_ffn[l]), ffn_w1[l].astype(BF16), ffn_w3[l].astype(BF16),
                  ffn_w2[l].astype(BF16), n_ctx_tiles, first_tile=n_ctx_tiles if last else 0)
    return xs
```

```python
import functools
import math

import jax
import jax.numpy as jnp
import numpy as np
from jax import lax
from jax.experimental import pallas as pl
from jax.experimental.pallas import tpu as pltpu

HEAD_DIM = 64
ATT_HEADS = 8
ATT_KV_HEADS = 2
GQA_GROUP = ATT_HEADS // ATT_KV_HEADS
ATT_Q_DIM = ATT_HEADS * HEAD_DIM
ATT_KV_DIM = ATT_KV_HEADS * HEAD_DIM
ATT_BLOCK = 128
ATT_SCALE = HEAD_DIM ** -0.5
ROPE_BASE = 10000.0
ROPE_FREQS = HEAD_DIM // 4
GRID_W = 64
RWKV_HEADS = 8
RWKV_HEAD = 64
RWKV_DIM = RWKV_HEADS * RWKV_HEAD
DECAY_LORA = 64
ICLR_LORA = 64
VRES_LORA = 32
GATE_LORA = 128
RWKV_IN = 3 * RWKV_DIM + DECAY_LORA + ICLR_LORA + GATE_LORA
GN_EPS = 64e-5
SGU_CHUNK = 128
SGU_GROUPS = 4
SGU_DIM = 512
N_BRANCH = 3
EPS = 1e-6

LANES = 128
TOKEN_TILE = 256
RWKV_CHUNK = 64
MXU_DIM = 256
RWKV_PACK = MXU_DIM // RWKV_HEAD
VMEM_LIMIT = 56 * 1024 * 1024

BF16 = jnp.bfloat16
F32 = jnp.float32


def _cparams(sem):
    return pltpu.CompilerParams(dimension_semantics=sem, vmem_limit_bytes=VMEM_LIMIT)


def _dot(a, b):
    return jnp.dot(a.astype(BF16), b.astype(BF16), preferred_element_type=F32)


def _split2(x):
    hi = x.astype(BF16)
    lo = (x - hi.astype(F32)).astype(BF16)
    return hi, lo


def _dot_hi_rhs_exact(x, m_bf16):
    hi, lo = _split2(x)
    return (jnp.dot(hi, m_bf16, preferred_element_type=F32)
            + jnp.dot(lo, m_bf16, preferred_element_type=F32))


def _rms_mod(x, gain, shift, scale):
    ms = jnp.mean(x * x, axis=-1, keepdims=True)
    return (x * lax.rsqrt(ms + EPS) * gain) * (1.0 + scale) + shift


def _mod_kernel(c_ref, w_ref, b_ref, o_ref):
    c = c_ref[...]
    o_ref[...] = _dot(c * jax.nn.sigmoid(c), w_ref[...]) + b_ref[...]


def _modulation(cc, w_mod, b_mod):
    nl, d, n6 = w_mod.shape
    rows = cc.shape[0]
    tn = 1536
    return pl.pallas_call(
        _mod_kernel,
        grid=(nl, n6 // tn),
        in_specs=[
            pl.BlockSpec((rows, d), lambda l, j: (0, 0)),
            pl.BlockSpec((None, d, tn), lambda l, j: (l, 0, j)),
            pl.BlockSpec((None, 1, tn), lambda l, j: (l, 0, j)),
        ],
        out_specs=pl.BlockSpec((None, rows, tn), lambda l, j: (l, 0, j)),
        out_shape=jax.ShapeDtypeStruct((nl, rows, n6), F32),
        compiler_params=_cparams(("parallel", "parallel")),
        name="modulation",
    )(cc, w_mod, b_mod.reshape(nl, 1, n6))


def _proj_in_kernel(has_vres, x_ref, mod_ref, gain_ref, w_ref, cos_ref, sin_ref, qg_ref, kg_ref, gm_ref,
                    lnw_ref, lnb_ref, sw_ref, sb_ref, q_ref, kv_ref, rw_ref, os_ref, *maybe_hd_ref):
    x = x_ref[...]
    mod = mod_ref[...]
    hb = _rms_mod(x, gain_ref[...], mod[0:1], mod[1:2]).astype(BF16)
    cos = cos_ref[...]
    sin = sin_ref[...]
    lane = lax.broadcasted_iota(jnp.int32, cos.shape, 1)
    first_half = (lane % HEAD_DIM) < (HEAD_DIM // 2)

    def norm_rope(p, gain):
        ms = _dot_hi_rhs_exact(p * p, gm_ref[...])
        t = p * lax.rsqrt(ms + EPS) * gain
        swapped = jnp.where(first_half, pltpu.roll(t, LANES - HEAD_DIM // 2, 1), pltpu.roll(t, HEAD_DIM // 2, 1))
        return t * cos + swapped * sin

    col = 0
    for j in range(ATT_Q_DIM // LANES):
        p = jnp.dot(hb, w_ref[:, col:col + LANES], preferred_element_type=F32)
        q_ref[:, j * LANES:(j + 1) * LANES] = norm_rope(p, qg_ref[...]).astype(BF16)
        col += LANES
    p = jnp.dot(hb, w_ref[:, col:col + LANES], preferred_element_type=F32)
    kv_ref[:, 0:LANES] = norm_rope(p, kg_ref[...]).astype(BF16)
    col += LANES
    kv_ref[:, LANES:2 * LANES] = jnp.dot(hb, w_ref[:, col:col + LANES], preferred_element_type=F32).astype(BF16)
    col += LANES
    step = 256
    for j in range(0, RWKV_IN, step):
        rw_ref[:, j:j + step] = jnp.dot(hb, w_ref[:, col + j:col + j + step], preferred_element_type=F32)
    col += RWKV_IN

    u = jax.nn.gelu(jnp.dot(hb, w_ref[:, col:col + SGU_DIM], preferred_element_type=F32))
    vg = jax.nn.gelu(jnp.dot(hb, w_ref[:, col + SGU_DIM:col + 2 * SGU_DIM], preferred_element_type=F32))
    col += 2 * SGU_DIM
    mu = jnp.mean(vg, axis=-1, keepdims=True)
    var = jnp.mean(jnp.square(vg - mu), axis=-1, keepdims=True)
    vn = ((vg - mu) * lax.rsqrt(var + EPS) * lnw_ref[...] + lnb_ref[...]).astype(BF16)
    gd = SGU_DIM // SGU_GROUPS
    for ck in range(x.shape[0] // SGU_CHUNK):
        rows = slice(ck * SGU_CHUNK, (ck + 1) * SGU_CHUNK)
        for g in range(SGU_GROUPS):
            cols = slice(g * gd, (g + 1) * gd)
            sp = jnp.dot(sw_ref[g], vn[rows, cols], preferred_element_type=F32) + sb_ref[g]
            os_ref[rows, cols] = (u[rows, cols] * sp).astype(BF16)
    if has_vres:
        maybe_hd_ref[0][...] = jnp.dot(hb, w_ref[:, col:col + LANES], preferred_element_type=F32)


def _proj_in(xs, modt, gain, w_ext, cos2, sin2, qg, kg, gmean, ln_w, ln_b, sgu_w, sgu_bb, n_ctx_tiles, has_vres):
    b, s, d = xs.shape
    tm = TOKEN_TILE
    nt = s // tm
    tok = lambda width: pl.BlockSpec((None, tm, width), lambda bi, i: (bi, i, 0))
    full = lambda a: pl.BlockSpec(a.shape, lambda bi, i: (0,) * a.ndim)
    out_shapes = [
        jax.ShapeDtypeStruct((b, s, ATT_Q_DIM), BF16),
        jax.ShapeDtypeStruct((b, s, 2 * ATT_KV_DIM), BF16),
        jax.ShapeDtypeStruct((b, s, RWKV_IN), F32),
        jax.ShapeDtypeStruct((b, s, SGU_DIM), BF16),
    ]
    out_specs = [tok(ATT_Q_DIM), tok(2 * ATT_KV_DIM), tok(RWKV_IN), tok(SGU_DIM)]
    if has_vres:
        out_shapes.append(jax.ShapeDtypeStruct((b, s, LANES), F32))
        out_specs.append(tok(LANES))
    return pl.pallas_call(
        functools.partial(_proj_in_kernel, has_vres),
        grid=(b, nt),
        in_specs=[
            tok(d),
            pl.BlockSpec((None, None, 6, d), lambda bi, i: (bi, jnp.where(i < n_ctx_tiles, 0, 1), 0, 0)),
            full(gain), full(w_ext),
            pl.BlockSpec((tm, LANES), lambda bi, i: (i, 0)),
            pl.BlockSpec((tm, LANES), lambda bi, i: (i, 0)),
            full(qg), full(kg), full(gmean), full(ln_w), full(ln_b), full(sgu_w), full(sgu_bb),
        ],
        out_specs=out_specs,
        out_shape=out_shapes,
        compiler_params=_cparams(("parallel", "parallel")),
        name="proj_in",
    )(xs, modt, gain, w_ext, cos2, sin2, qg, kg, gmean, ln_w, ln_b, sgu_w, sgu_bb)


def _attn_kernel(n_ctx_blocks, n_blocks, sink_ref, q_ref, kvc_ref, kvp_ref, kvo_ref, kvn_ref, o_ref):
    j = pl.program_id(1)
    blk = ATT_BLOCK
    is_lat = j >= n_ctx_blocks
    ok_prev = j >= n_ctx_blocks + 1
    ok_next = jnp.logical_and(is_lat, j <= n_blocks - 2)
    rows = GQA_GROUP * blk
    nband = 3 * blk
    qq = lax.broadcasted_iota(jnp.int32, (rows, nband), 0) % blk
    kcol = lax.broadcasted_iota(jnp.int32, (rows, nband), 1)
    kk = kcol % blk
    off_prev = jnp.where(ok_prev, 0, blk)
    off_next = jnp.where(ok_next, 0, blk)
    own_end = jnp.where(is_lat, 2 * blk, 0)
    band_mask = (((kcol < blk) & (kk >= qq + off_prev))
                 | ((kcol >= blk) & (kcol < own_end))
                 | ((kcol >= 2 * blk) & (kk <= qq - off_next)))
    rgrp = lax.broadcasted_iota(jnp.int32, (rows, 1), 0) // blk
    q = q_ref[...] * ATT_SCALE
    kvc, kvp, kvo, kvn = kvc_ref[...], kvp_ref[...], kvo_ref[...], kvn_ref[...]
    nt_dims = (((1,), (1,)), ((), ()))
    outs = []
    for kh in range(ATT_KV_HEADS):
        h0 = kh * GQA_GROUP
        qh = jnp.concatenate([q[:, (h0 + g) * HEAD_DIM:(h0 + g + 1) * HEAD_DIM] for g in range(GQA_GROUP)], axis=0)
        ksl = slice(kh * HEAD_DIM, (kh + 1) * HEAD_DIM)
        vsl = slice(ATT_KV_DIM + kh * HEAD_DIM, ATT_KV_DIM + (kh + 1) * HEAD_DIM)
        k_band = jnp.concatenate([kvp[:, ksl], kvo[:, ksl], kvn[:, ksl]], axis=0)
        v_band = jnp.concatenate([kvp[:, vsl], kvo[:, vsl], kvn[:, vsl]], axis=0)
        s_ctx = lax.dot_general(qh, kvc[:, ksl], nt_dims, preferred_element_type=F32)
        s_band = jnp.where(band_mask, lax.dot_general(qh, k_band, nt_dims, preferred_element_type=F32), -jnp.inf)
        sink = jnp.zeros((rows, 1), F32)
        for g in range(GQA_GROUP):
            sink = jnp.where(rgrp == g, sink_ref[h0 + g], sink)
        m = jnp.maximum(jnp.maximum(jnp.max(s_ctx, axis=-1, keepdims=True), jnp.max(s_band, axis=-1, keepdims=True)), sink)
        p_ctx = jnp.exp(s_ctx - m)
        p_band = jnp.exp(s_band - m)
        denom = (jnp.sum(p_ctx, axis=-1, keepdims=True) + jnp.sum(p_band, axis=-1, keepdims=True)) + jnp.exp(sink - m)
        o = (jnp.dot(p_ctx.astype(BF16), kvc[:, vsl], preferred_element_type=F32)
             + jnp.dot(p_band.astype(BF16), v_band, preferred_element_type=F32)) / denom
        outs.extend(o[g * blk:(g + 1) * blk] for g in range(GQA_GROUP))
    o_ref[...] = jnp.concatenate(outs, axis=1).astype(BF16)


def _attention(q, kv, sink, n_ctx):
    b, s, _ = q.shape
    blk = ATT_BLOCK
    nb = s // blk
    ncb = n_ctx // blk
    kvw = kv.shape[-1]
    return pl.pallas_call(
        functools.partial(_attn_kernel, ncb, nb),
        grid=(b, nb),
        in_specs=[
            pl.BlockSpec(memory_space=pltpu.SMEM),
            pl.BlockSpec((None, blk, ATT_Q_DIM), lambda bi, j: (bi, j, 0)),
            pl.BlockSpec((None, n_ctx, kvw), lambda bi, j: (bi, 0, 0)),
            pl.BlockSpec((None, blk, kvw), lambda bi, j: (bi, jnp.maximum(j - 1, 0), 0)),
            pl.BlockSpec((None, blk, kvw), lambda bi, j: (bi, j, 0)),
            pl.BlockSpec((None, blk, kvw), lambda bi, j: (bi, jnp.minimum(j + 1, nb - 1), 0)),
        ],
        out_specs=pl.BlockSpec((None, blk, ATT_Q_DIM), lambda bi, j: (bi, j, 0)),
        out_shape=jax.ShapeDtypeStruct((b, s, ATT_Q_DIM), BF16),
        compiler_params=_cparams(("parallel", "parallel")),
        name="window_attention",
    )(sink, q, kv, kv, kv, kv)


def _rwkv_tile(d, i, n_tiles):
    return jnp.where(d == 0, i, jnp.where(i == 0, 0, n_tiles - i))


def _rwkv_ops_kernel(has_vres, emit_v, n_ctx_tiles, n_tiles, *refs):
    it = iter(refs)
    rw_ref, rwp_ref, rwn_ref = next(it), next(it), next(it)
    vf_ref = hd_ref = vup_ref = vb_ref = None
    if has_vres:
        vf_ref, hd_ref = next(it), next(it)
    conv_ref, wa_ref, w0a0_ref, kkw_ref, kaw_ref, rkw_ref, gup_ref = (next(it) for _ in range(7))
    if has_vres:
        vup_ref, vb_ref = next(it), next(it)
    gsum_ref, tri_ref, bdm_ref = next(it), next(it), next(it)
    rm_ref, yc_ref, aux_ref = next(it), next(it), next(it)
    v_out_ref = next(it) if emit_v else None
    fa_ref, fr_ref, fb_ref, fk_ref, fbh_ref, fkh_ref, fv_ref, fel_ref = (next(it) for _ in range(8))

    t = pl.program_id(1)
    tm = rw_ref.shape[0]
    c_dim = RWKV_DIM
    lc = RWKV_CHUNK
    n_chunks = tm // lc
    gw = RWKV_PACK * RWKV_HEAD
    n_groups = c_dim // gw

    x = rw_ref[...]
    ok_prev = jnp.logical_and(t != 0, t != n_ctx_tiles)
    ok_next = jnp.logical_and(t != n_ctx_tiles - 1, t != n_tiles - 1)
    halo_p = jnp.where(ok_prev, rwp_ref[7:8, :], 0.0)
    halo_n = jnp.where(ok_next, rwn_ref[0:1, :], 0.0)
    row = lax.broadcasted_iota(jnp.int32, x.shape, 0)
    xp = jnp.where(row == 0, halo_p, pltpu.roll(x, 1, 0))
    xn = jnp.where(row == tm - 1, halo_n, pltpu.roll(x, tm - 1, 0))
    cw = conv_ref[...]
    cv = (xp * cw[0:1] + x * cw[1:2]) + xn * cw[2:3]
    r = cv[:, 0:c_dim]
    k = cv[:, c_dim:2 * c_dim]
    v = cv[:, 2 * c_dim:3 * c_dim]
    xwa = cv[:, 3 * c_dim:3 * c_dim + LANES]
    xg = cv[:, 3 * c_dim + LANES:3 * c_dim + 2 * LANES]

    if emit_v:
        v_out_ref[...] = v
    if has_vres:
        gate = jax.nn.sigmoid(vb_ref[...] + _dot(hd_ref[...], vup_ref[...]))
        v = v + (vf_ref[...] - v) * gate
    fv_ref[...] = v

    gsum = gsum_ref[...]
    kk = k * kkw_ref[...]
    kk = kk * lax.rsqrt(_dot_hi_rhs_exact(kk * kk, gsum) + 1e-12)
    lane = lax.broadcasted_iota(jnp.int32, xwa.shape, 1)
    lora_in = jnp.where(lane < DECAY_LORA, jnp.tanh(xwa), xwa).astype(BF16)
    kd_sum = None
    for d in range(2):
        pre = jnp.dot(lora_in, wa_ref[d], preferred_element_type=F32) + w0a0_ref[d]
        a_sig = jax.nn.sigmoid(pre[:, c_dim:2 * c_dim])
        lw = -jnp.exp(-jax.nn.softplus(-pre[:, 0:c_dim]) - 0.5)
        kd = k * (1.0 + (a_sig - 1.0) * kaw_ref[...])
        kd_sum = kd if d == 0 else kd_sum + kd
        bvec = kk * a_sig
        hi, lo = _split2(lw)
        cs = jnp.dot(tri_ref[d], hi, preferred_element_type=F32) + jnp.dot(tri_ref[d], lo, preferred_element_type=F32)
        c_inc = cs[0:tm]
        c_tot = cs[tm:2 * tm]
        e_inc = jnp.exp(c_inc)
        e_neg = jnp.exp(-c_inc)
        e_rem = jnp.exp(c_tot - c_inc)
        fa_ref[d] = -kk * jnp.exp(c_inc - lw)
        fr_ref[d] = r * e_inc
        fb_ref[d] = bvec * e_neg
        fk_ref[d] = kd * e_neg
        fbh_ref[d] = bvec * e_rem
        fkh_ref[d] = kd * e_rem
        for c in range(n_chunks):
            fel_ref[d, c:c + 1, :] = jnp.exp(c_tot[c * lc:c * lc + 1, :])
    aux_ref[:, 0:c_dim] = _dot_hi_rhs_exact(r * kd_sum * rkw_ref[...], gsum) * v
    aux_ref[:, c_dim:2 * c_dim] = _dot(jax.nn.sigmoid(xg), gup_ref[...])

    bdm = bdm_ref[...]

    def bd(xb):
        return jnp.concatenate([xb] * RWKV_PACK, axis=0) * bdm

    low_half = lax.broadcasted_iota(jnp.int32, (lc, 2 * lc), 1) < lc

    def head_transpose(xf):
        w = jnp.concatenate([xf, xf], axis=0).T
        pairs = [jnp.where(low_half, w[(2 * j) * lc:(2 * j + 1) * lc], w[(2 * j + 1) * lc:(2 * j + 2) * lc])
                 for j in range(RWKV_PACK // 2)]
        return jnp.concatenate(pairs, axis=1)

    mm = lambda p, q: jnp.dot(p, q, preferred_element_type=F32)
    ri = lax.broadcasted_iota(jnp.int32, (2 * lc, 2 * gw), 0)
    ci = lax.broadcasted_iota(jnp.int32, (2 * lc, 2 * gw), 1) % lc
    amask = [jnp.logical_or(jnp.logical_and(ri < lc, sgn * (ri - ci) > 0),
                            jnp.logical_and(ri >= lc, sgn * (ri - lc - ci) >= 0)) for sgn in (1, -1)]
    eye_p = (lax.broadcasted_iota(jnp.int32, (lc, gw), 0)
             == lax.broadcasted_iota(jnp.int32, (lc, gw), 1) % lc).astype(F32)
    n_levels = int(math.log2(lc))

    probs = [(d, c, g) for d in range(2) for c in range(n_chunks) for g in range(n_groups)]
    blk = lambda ref, p: ref[p[0], p[1] * lc:(p[1] + 1) * lc, p[2] * gw:(p[2] + 1) * gw]
    vblk = lambda p: fv_ref[p[1] * lc:(p[1] + 1) * lc, p[2] * gw:(p[2] + 1) * gw]
    ab = [blk(fa_ref, p).astype(BF16) for p in probs]
    vb = [vblk(p).astype(BF16) for p in probs]
    a_all = []
    for j, p in enumerate(probs):
        lhs = jnp.concatenate([ab[j], blk(fr_ref, p).astype(BF16)], axis=0)
        rhs_t = jnp.concatenate([bd(blk(fb_ref, p).astype(BF16)), bd(blk(fk_ref, p).astype(BF16))], axis=0)
        full = lax.dot_general(lhs, rhs_t, (((1,), (1,)), ((), ())), preferred_element_type=F32)
        a_all.append(jnp.where(amask[p[0]], full, 0.0))
    a_rb = [m[lc:2 * lc, 0:gw].astype(BF16) for m in a_all]
    a_rk = [m[lc:2 * lc, gw:2 * gw].astype(BF16) for m in a_all]
    t_p = [eye_p + m[0:lc, 0:gw] for m in a_all]
    pb = [m[0:lc, 0:gw].astype(BF16) for m in a_all]
    pb = [mm(p, bd(p)).astype(BF16) for p in pb]
    for lvl in range(1, n_levels):
        if lvl < n_levels - 1:
            res = [mm(jnp.concatenate([p, t.astype(BF16)], axis=0), bd(p)) for p, t in zip(pb, t_p)]
            pb = [m[0:lc].astype(BF16) for m in res]
            t_p = [t + m[lc:2 * lc] for t, m in zip(t_p, res)]
        else:
            t_p = [t + mm(t.astype(BF16), bd(p)) for p, t in zip(pb, t_p)]
    u = [mm(m[0:lc, gw:2 * gw].astype(BF16), bd(vv)) for m, vv in zip(a_all, vb)]
    xs = [mm(t.astype(BF16), jnp.concatenate([bd(a), bd(uu.astype(BF16))], axis=1))
          for t, a, uu in zip(t_p, ab, u)]
    a1b = [m[:, 0:gw].astype(BF16) for m in xs]
    u1b = [m[:, gw:2 * gw].astype(BF16) for m in xs]
    for j, p in enumerate(probs):
        d, c, g = p
        zb = head_transpose(blk(fbh_ref, p)).astype(BF16)
        zk = head_transpose(blk(fkh_ref, p)).astype(BF16)
        res = mm(jnp.concatenate([a_rb[j], zb], axis=0), bd(a1b[j]))
        el_c = fel_ref[d, c:c + 1, g * gw:(g + 1) * gw]
        rm_ref[d, c, g] = jnp.concatenate([blk(fr_ref, p) + res[0:lc], res[lc:2 * lc] + eye_p * el_c],
                                          axis=0).astype(BF16)
        yc_ref[d, c, g] = mm(jnp.concatenate([jnp.concatenate([a_rb[j], a_rk[j]], axis=1),
                                              jnp.concatenate([zb, zk], axis=1)], axis=0),
                             jnp.concatenate([bd(u1b[j]), bd(vb[j])], axis=0))


def _rwkv_carry_kernel(n_tiles, rm_ref, yc_ref, bdm_ref, y_ref, st_ref):
    d = pl.program_id(0)
    i = pl.program_id(1)
    nb, n_chunks, n_groups, lc2, gw = rm_ref.shape
    lc = lc2 // 2
    bdm = bdm_ref[...]

    @pl.when(i == 0)
    def _():
        st_ref[...] = jnp.zeros_like(st_ref)

    for n in range(n_chunks):
        cc = jnp.where(d == 0, n, n_chunks - 1 - n)
        rows = pl.ds(pl.multiple_of(cc * lc, lc), lc)
        for bi in range(nb):
            for g in range(n_groups):
                st_bd = jnp.concatenate([st_ref[bi, g].astype(BF16)] * RWKV_PACK, axis=0) * bdm
                tot = jnp.dot(rm_ref[bi, cc, g], st_bd, preferred_element_type=F32) + yc_ref[bi, cc, g]
                y_ref[bi, rows, g * gw:(g + 1) * gw] = tot[0:lc]
                st_ref[bi, g] = tot[lc:2 * lc]


def _rwkv(rw, conv_w, wa, w0a0, k_k, k_a, r_k, g_up, gsum, tri, bdm, n_ctx_tiles, emit_v, vres=None):
    b, s, _ = rw.shape
    tm = TOKEN_TILE
    nt = s // tm
    c = RWKV_DIM
    lc = RWKV_CHUNK
    gw = RWKV_PACK * RWKV_HEAD
    n_groups, n_chunks = c // gw, tm // lc
    halo_blocks = tm // 8
    last_halo = s // 8 - 1
    tok = lambda width: pl.BlockSpec((None, tm, width), lambda bi, t: (bi, t, 0))
    full = lambda a: pl.BlockSpec(a.shape, lambda bi, t: (0,) * a.ndim)
    in_specs = [
        tok(RWKV_IN),
        pl.BlockSpec((None, 8, RWKV_IN), lambda bi, t: (bi, jnp.maximum(t * halo_blocks - 1, 0), 0)),
        pl.BlockSpec((None, 8, RWKV_IN), lambda bi, t: (bi, jnp.minimum((t + 1) * halo_blocks, last_halo), 0)),
    ]
    args = [rw, rw, rw]
    if vres is not None:
        v_first, hd, vup, vb = vres
        in_specs += [tok(c), tok(LANES)]
        args += [v_first, hd]
    in_specs += [full(conv_w), full(wa), full(w0a0), full(k_k), full(k_a), full(r_k), full(g_up)]
    args += [conv_w, wa, w0a0, k_k, k_a, r_k, g_up]
    if vres is not None:
        in_specs += [full(vup), full(vb)]
        args += [vup, vb]
    in_specs += [full(gsum), full(tri), full(bdm)]
    args += [gsum, tri, bdm]
    op_shape = (2, b, nt, n_chunks, n_groups, 2 * lc, gw)
    op_spec = pl.BlockSpec((2, None, None, n_chunks, n_groups, 2 * lc, gw), lambda bi, t: (0, bi, t, 0, 0, 0, 0))
    out_specs = [op_spec, op_spec, tok(2 * c)]
    out_shape = [jax.ShapeDtypeStruct(op_shape, BF16), jax.ShapeDtypeStruct(op_shape, F32),
                 jax.ShapeDtypeStruct((b, s, 2 * c), F32)]
    if emit_v:
        out_specs.append(tok(c))
        out_shape.append(jax.ShapeDtypeStruct((b, s, c), F32))
    feat_buf = pltpu.VMEM((2, tm, c), F32)
    scratch = [feat_buf] * 6 + [pltpu.VMEM((tm, c), F32), pltpu.VMEM((2, 8 * (-(-n_chunks // 8)), c), F32)]
    outs = pl.pallas_call(
        functools.partial(_rwkv_ops_kernel, vres is not None, emit_v, n_ctx_tiles, nt),
        grid=(b, nt),
        in_specs=in_specs,
        out_specs=out_specs,
        out_shape=out_shape,
        scratch_shapes=scratch,
        compiler_params=_cparams(("parallel", "parallel")),
        name="rwkv7_chunk_ops",
    )(*args)
    rm, yc = outs[0], outs[1]

    tile = lambda d, i: _rwkv_tile(d, i, nt)
    carry_in = pl.BlockSpec((None, b, None, n_chunks, n_groups, 2 * lc, gw), lambda d, i: (d, 0, tile(d, i), 0, 0, 0, 0))
    y = pl.pallas_call(
        functools.partial(_rwkv_carry_kernel, nt),
        grid=(2, nt),
        in_specs=[carry_in, carry_in, pl.BlockSpec(bdm.shape, lambda d, i: (0, 0))],
        out_specs=pl.BlockSpec((None, b, tm, c), lambda d, i: (d, 0, tile(d, i), 0)),
        out_shape=jax.ShapeDtypeStruct((2, b, s, c), F32),
        scratch_shapes=[pltpu.VMEM((b, n_groups, lc, gw), F32)],
        compiler_params=_cparams(("parallel", "arbitrary")),
        name="rwkv7_state_carry",
    )(rm, yc, bdm)
    return (y,) + tuple(outs[2:])


def _merge_kernel(x_ref, mod_ref, gain_ref, wgt_ref, oa_ref, y_ref, aux_ref, os_ref, lnw_ref, lnb_ref, gm_ref,
                  woa_ref, wor_ref, wos_ref, wout_ref, o_ref):
    d_model = x_ref.shape[-1]
    c = RWKV_DIM
    x = x_ref[...]
    mod = mod_ref[...]
    hb = _rms_mod(x, gain_ref[...], mod[0:1], mod[1:2]).astype(BF16)
    y = y_ref[0] + y_ref[1]
    gm = gm_ref[...]
    mu = _dot_hi_rhs_exact(y, gm)
    yc = y - mu
    var = _dot_hi_rhs_exact(yc * yc, gm)
    yn = yc * lax.rsqrt(var + GN_EPS) * lnw_ref[...] + lnb_ref[...]
    o_rwkv = ((yn + aux_ref[:, 0:c]) * aux_ref[:, c:2 * c]).astype(BF16)
    gate = lambda j: jax.nn.sigmoid(jnp.dot(hb, wgt_ref[:, j * d_model:(j + 1) * d_model], preferred_element_type=F32))
    m = (gate(0) * jnp.dot(oa_ref[...], woa_ref[...], preferred_element_type=F32)
         + gate(1) * jnp.dot(o_rwkv, wor_ref[...], preferred_element_type=F32)
         + gate(2) * jnp.dot(os_ref[...], wos_ref[...], preferred_element_type=F32))
    mix = _dot(m, wout_ref[...])
    o_ref[...] = x + mod[2:3] * mix


def _merge(xs, modt, gain, w_gt, o_attn, y, aux, o_sgu, ln_w, ln_b, gmean, woa, wor, wos, wout, n_ctx_tiles):
    b, s, d = xs.shape
    tm = TOKEN_TILE
    tok = lambda width: pl.BlockSpec((None, tm, width), lambda bi, i: (bi, i, 0))
    tok2 = lambda width: pl.BlockSpec((2, None, tm, width), lambda bi, i: (0, bi, i, 0))
    full = lambda a: pl.BlockSpec(a.shape, lambda bi, i: (0,) * a.ndim)
    return pl.pallas_call(
        _merge_kernel,
        grid=(b, s // tm),
        in_specs=[
            tok(d),
            pl.BlockSpec((None, None, 6, d), lambda bi, i: (bi, jnp.where(i < n_ctx_tiles, 0, 1), 0, 0)),
            full(gain), full(w_gt),
            tok(ATT_Q_DIM), tok2(RWKV_DIM), tok(2 * RWKV_DIM), tok(SGU_DIM),
            full(ln_w), full(ln_b), full(gmean), full(woa), full(wor), full(wos), full(wout),
        ],
        out_specs=tok(d),
        out_shape=jax.ShapeDtypeStruct((b, s, d), F32),
        compiler_params=_cparams(("parallel", "parallel")),
        name="merge_out",
    )(xs, modt, gain, w_gt, o_attn, y, aux, o_sgu, ln_w, ln_b, gmean, woa, wor, wos, wout)


def _ffn_kernel(x_ref, mod_ref, gain_ref, w1_ref, w3_ref, w2_ref, o_ref):
    x = x_ref[...]
    mod = mod_ref[...]
    hb = _rms_mod(x, gain_ref[...], mod[3:4], mod[4:5]).astype(BF16)
    a = jnp.dot(hb, w1_ref[...], preferred_element_type=F32)
    bq = jnp.dot(hb, w3_ref[...], preferred_element_type=F32)
    hid = (a * jax.nn.sigmoid(a)) * bq
    o_ref[...] = x + mod[5:6] * _dot(hid, w2_ref[...])


def _ffn(xs, modt, gain, w1, w3, w2, n_ctx_tiles, first_tile):
    b, s, d = xs.shape
    tm = TOKEN_TILE
    nt = s // tm - first_tile
    full = lambda a: pl.BlockSpec(a.shape, lambda bi, i: (0,) * a.ndim)
    return pl.pallas_call(
        _ffn_kernel,
        grid=(b, nt),
        in_specs=[
            pl.BlockSpec((None, tm, d), lambda bi, i: (bi, i + first_tile, 0)),
            pl.BlockSpec((None, None, 6, d), lambda bi, i: (bi, jnp.where(i + first_tile < n_ctx_tiles, 0, 1), 0, 0)),
            full(gain), full(w1), full(w3), full(w2),
        ],
        out_specs=pl.BlockSpec((None, tm, d), lambda bi, i: (bi, i, 0)),
        out_shape=jax.ShapeDtypeStruct((b, nt * tm, d), F32),
        compiler_params=_cparams(("parallel", "parallel")),
        name="swiglu_ffn",
    )(xs, modt, gain, w1, w3, w2)


def _rope_tables(n_ctx, n_lat):
    rows = n_lat // GRID_W
    row = jnp.repeat(jnp.arange(rows), GRID_W).astype(F32)
    col = jnp.tile(jnp.arange(GRID_W), rows).astype(F32)
    inv = ROPE_BASE ** (-jnp.arange(ROPE_FREQS, dtype=F32) / ROPE_FREQS)
    ang = jnp.concatenate([row[:, None] * inv, col[:, None] * inv], -1)
    cos, sin = jnp.cos(ang), jnp.sin(ang)
    cos = jnp.concatenate([jnp.ones((n_ctx, HEAD_DIM // 2), F32), cos], 0)
    sin = jnp.concatenate([jnp.zeros((n_ctx, HEAD_DIM // 2), F32), sin], 0)
    return jnp.tile(jnp.concatenate([cos, cos], -1), (1, 2)), jnp.tile(jnp.concatenate([-sin, sin], -1), (1, 2))


def _block_diag_ones(n, blk, scale):
    i = np.arange(n)
    return jnp.asarray(((i[:, None] // blk) == (i[None, :] // blk)).astype(np.float32) * scale, dtype=BF16)


def _cumsum_matrices(tm, lc):
    i = np.arange(tm)
    same = (i[:, None] // lc) == (i[None, :] // lc)
    fwd = same & (i[None, :] <= i[:, None])
    bwd = same & (i[None, :] >= i[:, None])
    mats = np.stack([np.concatenate([fwd, same], 0), np.concatenate([bwd, same], 0)]).astype(np.float32)
    return jnp.asarray(mats, dtype=BF16)


def kernel(x, c, ctx, c_ctx, w_mod, b_mod, norm_mix, norm_ffn, w_in, q_gain, k_gain, attn_sink,
           rwkv_conv, rwkv_w0, rwkv_w_up, rwkv_a0, rwkv_a_up, rwkv_k_k, rwkv_k_a, rwkv_r_k, rwkv_g_up,
           rwkv_ln_w, rwkv_ln_b, rwkv_vres_down, rwkv_vres_up, rwkv_vres_b,
           sgu_ln_w, sgu_ln_b, sgu_w, sgu_b, w_o_attn, w_o_rwkv, w_o_sgu, w_out,
           ffn_w1, ffn_w3, ffn_w2):
    b, n_lat, d = x.shape
    n_ctx = ctx.shape[1]
    depth = w_in.shape[0]
    tm = TOKEN_TILE
    assert n_ctx % tm == 0 and n_lat % tm == 0 and n_lat % GRID_W == 0 and 2 * RWKV_CHUNK == LANES
    n_ctx_tiles = n_ctx // tm
    cdim = RWKV_DIM

    xs = jnp.concatenate([ctx, x], axis=1)
    cos2, sin2 = _rope_tables(n_ctx, n_lat)
    gmean = _block_diag_ones(cdim, HEAD_DIM, 1.0 / HEAD_DIM)
    gmean2 = gmean[:LANES, :LANES]
    gsum = _block_diag_ones(cdim, RWKV_HEAD, 1.0)
    tri = _cumsum_matrices(tm, RWKV_CHUNK)
    bdm = _block_diag_ones(RWKV_PACK * RWKV_HEAD, RWKV_HEAD, 1.0)

    rows = -(-(b + 1) // 8) * 8
    cc = jnp.zeros((rows, d), F32).at[:b].set(c).at[b].set(c_ctx)
    mods = _modulation(cc, w_mod, b_mod)

    row2 = lambda a: a.reshape(1, -1)
    v_first = None
    for l in range(depth):
        last = l == depth - 1
        has_vres = l > 0
        mod_lat = mods[l, :b].reshape(b, 1, 6, d)
        mod_ctx = jnp.broadcast_to(mods[l, b].reshape(1, 1, 6, d), (b, 1, 6, d))
        modt = jnp.concatenate([mod_ctx, mod_lat], axis=1)

        n_proj = w_in.shape[2] - N_BRANCH * d
        w_ext = w_in[l, :, :n_proj]
        if has_vres:
            down = jnp.pad(rwkv_vres_down[l - 1], ((0, 0), (0, LANES - VRES_LORA)))
            w_ext = jnp.concatenate([w_ext, down], axis=1)
        w_ext = w_ext.astype(BF16)
        w_gt = w_in[l, :, n_proj:].astype(BF16)
        qg2 = jnp.tile(q_gain[l], 2).reshape(1, LANES)
        kg2 = jnp.tile(k_gain[l], 2).reshape(1, LANES)
        sgu_bb = jnp.broadcast_to(sgu_b[l][:, :, None], (SGU_GROUPS, SGU_CHUNK, SGU_DIM // SGU_GROUPS))
        outs = _proj_in(xs, modt, row2(norm_mix[l]), w_ext, cos2, sin2, qg2, kg2, gmean2,
                        row2(sgu_ln_w[l]), row2(sgu_ln_b[l]), sgu_w[l].astype(BF16), sgu_bb, n_ctx_tiles, has_vres)
        q, kv, rw, o_sgu = outs[:4]

        o_attn = _attention(q, kv, attn_sink[l], n_ctx)

        zeros_lora = jnp.zeros((2, DECAY_LORA, cdim), F32)
        wa = jnp.concatenate([jnp.concatenate([rwkv_w_up[l], zeros_lora], axis=2),
                              jnp.concatenate([zeros_lora, rwkv_a_up[l]], axis=2)], axis=1).astype(BF16)
        w0a0 = jnp.concatenate([rwkv_w0[l], rwkv_a0[l]], axis=1).reshape(2, 1, 2 * cdim)
        vres = None
        if has_vres:
            vup = jnp.pad(rwkv_vres_up[l - 1], ((0, LANES - VRES_LORA), (0, 0))).astype(BF16)
            vres = (v_first, outs[4], vup, row2(rwkv_vres_b[l - 1]))
        r_outs = _rwkv(rw, rwkv_conv[l], wa, w0a0, row2(rwkv_k_k[l]), row2(rwkv_k_a[l]), row2(rwkv_r_k[l]),
                       rwkv_g_up[l].astype(BF16), gsum, tri, bdm, n_ctx_tiles, emit_v=(l == 0), vres=vres)
        y, aux = r_outs[0], r_outs[1]
        if l == 0:
            v_first = r_outs[2]

        xs = _merge(xs, modt, row2(norm_mix[l]), w_gt, o_attn, y, aux, o_sgu, row2(rwkv_ln_w[l]), row2(rwkv_ln_b[l]),
                    gmean, w_o_attn[l].astype(BF16), w_o_rwkv[l].astype(BF16), w_o_sgu[l].astype(BF16),
                    w_out[l].astype(BF16), n_ctx_tiles)
        xs = _ffn(xs, modt, row2(norm_ffn[l]), ffn_w1[l].astype(BF16), ffn_w3[l].astype(BF16),
                  ffn_w2[l].astype(BF16), n_ctx_tiles, first_tile=n_ctx_tiles if last else 0)
    return xs
```

```python
import functools
import math

import jax
import jax.numpy as jnp
import numpy as np
from jax import lax
from jax.experimental import pallas as pl
from jax.experimental.pallas import tpu as pltpu

HEAD_DIM = 64
ATT_HEADS = 8
ATT_KV_HEADS = 2
GQA_GROUP = ATT_HEADS // ATT_KV_HEADS
ATT_Q_DIM = ATT_HEADS * HEAD_DIM
ATT_KV_DIM = ATT_KV_HEADS * HEAD_DIM
ATT_BLOCK = 128
ATT_SCALE = HEAD_DIM ** -0.5
ROPE_BASE = 10000.0
ROPE_FREQS = HEAD_DIM // 4
GRID_W = 64
RWKV_HEADS = 8
RWKV_HEAD = 64
RWKV_DIM = RWKV_HEADS * RWKV_HEAD
DECAY_LORA = 64
ICLR_LORA = 64
VRES_LORA = 32
GATE_LORA = 128
RWKV_IN = 3 * RWKV_DIM + DECAY_LORA + ICLR_LORA + GATE_LORA
GN_EPS = 64e-5
SGU_CHUNK = 128
SGU_GROUPS = 4
SGU_DIM = 512
N_BRANCH = 3
EPS = 1e-6

LANES = 128
TOKEN_TILE = 256
RWKV_CHUNK = 64
MXU_DIM = 256
RWKV_PACK = MXU_DIM // RWKV_HEAD
VMEM_LIMIT = 56 * 1024 * 1024

BF16 = jnp.bfloat16
F32 = jnp.float32


def _cparams(sem):
    return pltpu.CompilerParams(dimension_semantics=sem, vmem_limit_bytes=VMEM_LIMIT)


def _dot(a, b):
    return jnp.dot(a.astype(BF16), b.astype(BF16), preferred_element_type=F32)


def _split2(x):
    hi = x.astype(BF16)
    lo = (x - hi.astype(F32)).astype(BF16)
    return hi, lo


def _head_sums(x, g_bf16):
    w = g_bf16.shape[0]
    hi, lo = _split2(x)
    parts = [jnp.dot(hi[:, j:j + w], g_bf16, preferred_element_type=F32)
             + jnp.dot(lo[:, j:j + w], g_bf16, preferred_element_type=F32) for j in range(0, x.shape[1], w)]
    return parts[0] if len(parts) == 1 else jnp.concatenate(parts, axis=1)


def _alternate(*gens):
    live = list(gens)
    while live:
        for g in list(live):
            try:
                next(g)
            except StopIteration:
                live.remove(g)


def _rms_mod(x, gain, shift, scale):
    ms = jnp.mean(x * x, axis=-1, keepdims=True)
    return (x * lax.rsqrt(ms + EPS) * gain) * (1.0 + scale) + shift


def _mod_kernel(c_ref, w_ref, b_ref, o_ref):
    c = c_ref[...]
    o_ref[...] = _dot(c * jax.nn.sigmoid(c), w_ref[...]) + b_ref[...]


def _modulation(cc, w_mod, b_mod):
    nl, d, n6 = w_mod.shape
    rows = cc.shape[0]
    tn = 1536
    return pl.pallas_call(
        _mod_kernel,
        grid=(nl, n6 // tn),
        in_specs=[
            pl.BlockSpec((rows, d), lambda l, j: (0, 0)),
            pl.BlockSpec((None, d, tn), lambda l, j: (l, 0, j)),
            pl.BlockSpec((None, 1, tn), lambda l, j: (l, 0, j)),
        ],
        out_specs=pl.BlockSpec((None, rows, tn), lambda l, j: (l, 0, j)),
        out_shape=jax.ShapeDtypeStruct((nl, rows, n6), F32),
        compiler_params=_cparams(("parallel", "parallel")),
        name="modulation",
    )(cc, w_mod, b_mod.reshape(nl, 1, n6))


def _proj_in_kernel(has_vres, x_ref, mod_ref, gain_ref, w_ref, cos_ref, sin_ref, qg_ref, kg_ref, gm_ref,
                    lnw_ref, lnb_ref, sw_ref, sb_ref, q_ref, kv_ref, rw_ref, os_ref, *maybe_hd_ref):
    x = x_ref[...]
    mod = mod_ref[...]
    hb = _rms_mod(x, gain_ref[...], mod[0:1], mod[1:2]).astype(BF16)
    cos = cos_ref[...]
    sin = sin_ref[...]
    gm = gm_ref[...]

    def norm_rope(p, gain):
        reps = p.shape[1] // LANES
        tile = lambda a: a if reps == 1 else jnp.concatenate([a] * reps, axis=1)
        lane = lax.broadcasted_iota(jnp.int32, p.shape, 1)
        first_half = (lane % HEAD_DIM) < (HEAD_DIM // 2)
        ms = _head_sums(p * p, gm[:p.shape[1], :p.shape[1]])
        t = p * lax.rsqrt(ms + EPS) * tile(gain)
        swapped = jnp.where(first_half, pltpu.roll(t, p.shape[1] - HEAD_DIM // 2, 1), pltpu.roll(t, HEAD_DIM // 2, 1))
        return t * tile(cos) + swapped * tile(sin)

    proj = lambda c0, width: jnp.dot(hb, w_ref[:, c0:c0 + width], preferred_element_type=F32)
    c_q, c_kv = 0, ATT_Q_DIM
    c_rw = c_kv + 2 * ATT_KV_DIM
    c_sg = c_rw + RWKV_IN
    c_hd = c_sg + 2 * SGU_DIM
    gd = SGU_DIM // SGU_GROUPS

    def rwkv_columns():
        for j in range(0, RWKV_IN, MXU_DIM):
            rw_ref[:, j:j + MXU_DIM] = proj(c_rw + j, MXU_DIM)
            yield
        if has_vres:
            maybe_hd_ref[0][...] = proj(c_hd, LANES)
        yield

    def epilogues():
        pq = [proj(c_q + j, MXU_DIM) for j in range(0, ATT_Q_DIM, MXU_DIM)]
        pkv = proj(c_kv, 2 * ATT_KV_DIM)
        yield
        pu = proj(c_sg, SGU_DIM)
        for j, p in enumerate(pq):
            q_ref[:, j * MXU_DIM:(j + 1) * MXU_DIM] = norm_rope(p, qg_ref[...]).astype(BF16)
            yield
        pvg = proj(c_sg + SGU_DIM, SGU_DIM)
        kv_ref[:, 0:ATT_KV_DIM] = norm_rope(pkv[:, 0:ATT_KV_DIM], kg_ref[...]).astype(BF16)
        kv_ref[:, ATT_KV_DIM:2 * ATT_KV_DIM] = pkv[:, ATT_KV_DIM:2 * ATT_KV_DIM].astype(BF16)
        yield
        u = jax.nn.gelu(pu)
        yield
        vg = jax.nn.gelu(pvg)
        yield
        mu = jnp.mean(vg, axis=-1, keepdims=True)
        var = jnp.mean(jnp.square(vg - mu), axis=-1, keepdims=True)
        vn = ((vg - mu) * lax.rsqrt(var + EPS) * lnw_ref[...] + lnb_ref[...]).astype(BF16)
        yield
        for ck in range(x.shape[0] // SGU_CHUNK):
            rows = slice(ck * SGU_CHUNK, (ck + 1) * SGU_CHUNK)
            for g in range(SGU_GROUPS):
                cols = slice(g * gd, (g + 1) * gd)
                sp = jnp.dot(sw_ref[g], vn[rows, cols], preferred_element_type=F32) + sb_ref[g]
                os_ref[rows, cols] = (u[rows, cols] * sp).astype(BF16)
            yield

    _alternate(epilogues(), rwkv_columns())


def _proj_in(xs, modt, gain, w_ext, cos2, sin2, qg, kg, gmean, ln_w, ln_b, sgu_w, sgu_bb, n_ctx_tiles, has_vres):
    b, s, d = xs.shape
    tm = TOKEN_TILE
    nt = s // tm
    tok = lambda width: pl.BlockSpec((None, tm, width), lambda bi, i: (bi, i, 0))
    full = lambda a: pl.BlockSpec(a.shape, lambda bi, i: (0,) * a.ndim)
    out_shapes = [
        jax.ShapeDtypeStruct((b, s, ATT_Q_DIM), BF16),
        jax.ShapeDtypeStruct((b, s, 2 * ATT_KV_DIM), BF16),
        jax.ShapeDtypeStruct((b, s, RWKV_IN), F32),
        jax.ShapeDtypeStruct((b, s, SGU_DIM), BF16),
    ]
    out_specs = [tok(ATT_Q_DIM), tok(2 * ATT_KV_DIM), tok(RWKV_IN), tok(SGU_DIM)]
    if has_vres:
        out_shapes.append(jax.ShapeDtypeStruct((b, s, LANES), F32))
        out_specs.append(tok(LANES))
    return pl.pallas_call(
        functools.partial(_proj_in_kernel, has_vres),
        grid=(b, nt),
        in_specs=[
            tok(d),
            pl.BlockSpec((None, None, 6, d), lambda bi, i: (bi, jnp.where(i < n_ctx_tiles, 0, 1), 0, 0)),
            full(gain), full(w_ext),
            pl.BlockSpec((tm, LANES), lambda bi, i: (i, 0)),
            pl.BlockSpec((tm, LANES), lambda bi, i: (i, 0)),
            full(qg), full(kg), full(gmean), full(ln_w), full(ln_b), full(sgu_w), full(sgu_bb),
        ],
        out_specs=out_specs,
        out_shape=out_shapes,
        compiler_params=_cparams(("parallel", "parallel")),
        name="proj_in",
    )(xs, modt, gain, w_ext, cos2, sin2, qg, kg, gmean, ln_w, ln_b, sgu_w, sgu_bb)


def _attn_kernel(n_ctx_blocks, n_blocks, sink_ref, q_ref, kvc_ref, kvp_ref, kvo_ref, kvn_ref, o_ref):
    j = pl.program_id(1)
    blk = ATT_BLOCK
    is_lat = j >= n_ctx_blocks
    ok_prev = j >= n_ctx_blocks + 1
    ok_next = jnp.logical_and(is_lat, j <= n_blocks - 2)
    rows = GQA_GROUP * blk
    nband = 3 * blk
    qq = lax.broadcasted_iota(jnp.int32, (blk, nband), 0)
    kcol = lax.broadcasted_iota(jnp.int32, (blk, nband), 1)
    kk = kcol % blk
    off_prev = jnp.where(ok_prev, 0, blk)
    off_next = jnp.where(ok_next, 0, blk)
    own_end = jnp.where(is_lat, 2 * blk, 0)
    band_mask = (((kcol < blk) & (kk >= qq + off_prev))
                 | ((kcol >= blk) & (kcol < own_end))
                 | ((kcol >= 2 * blk) & (kk <= qq - off_next)))
    band_bias = jnp.where(band_mask, 0.0, -jnp.inf)
    band_bias = jnp.concatenate([band_bias] * GQA_GROUP, axis=0)
    rgrp = lax.broadcasted_iota(jnp.int32, (rows, 1), 0) // blk
    q = q_ref[...] * ATT_SCALE
    kvc, kvp, kvo, kvn = kvc_ref[...], kvp_ref[...], kvo_ref[...], kvn_ref[...]
    nt_dims = (((1,), (1,)), ((), ()))
    outs = []
    for kh in range(ATT_KV_HEADS):
        h0 = kh * GQA_GROUP
        qh = jnp.concatenate([q[:, (h0 + g) * HEAD_DIM:(h0 + g + 1) * HEAD_DIM] for g in range(GQA_GROUP)], axis=0)
        ksl = slice(kh * HEAD_DIM, (kh + 1) * HEAD_DIM)
        vsl = slice(ATT_KV_DIM + kh * HEAD_DIM, ATT_KV_DIM + (kh + 1) * HEAD_DIM)
        k_band = jnp.concatenate([kvp[:, ksl], kvo[:, ksl], kvn[:, ksl]], axis=0)
        v_band = jnp.concatenate([kvp[:, vsl], kvo[:, vsl], kvn[:, vsl]], axis=0)
        s_ctx = lax.dot_general(qh, kvc[:, ksl], nt_dims, preferred_element_type=F32)
        s_band = lax.dot_general(qh, k_band, nt_dims, preferred_element_type=F32) + band_bias
        sink = jnp.zeros((rows, 1), F32)
        for g in range(GQA_GROUP):
            sink = jnp.where(rgrp == g, sink_ref[h0 + g], sink)
        m = jnp.maximum(jnp.maximum(jnp.max(s_ctx, axis=-1, keepdims=True), jnp.max(s_band, axis=-1, keepdims=True)), sink)
        p_ctx = jnp.exp(s_ctx - m)
        p_band = jnp.exp(s_band - m)
        denom = (jnp.sum(p_ctx, axis=-1, keepdims=True) + jnp.sum(p_band, axis=-1, keepdims=True)) + jnp.exp(sink - m)
        o = (jnp.dot(p_ctx.astype(BF16), kvc[:, vsl], preferred_element_type=F32)
             + jnp.dot(p_band.astype(BF16), v_band, preferred_element_type=F32)) / denom
        outs.extend(o[g * blk:(g + 1) * blk] for g in range(GQA_GROUP))
    o_ref[...] = jnp.concatenate(outs, axis=1).astype(BF16)


def _attention(q, kv, sink, n_ctx):
    b, s, _ = q.shape
    blk = ATT_BLOCK
    nb = s // blk
    ncb = n_ctx // blk
    kvw = kv.shape[-1]
    return pl.pallas_call(
        functools.partial(_attn_kernel, ncb, nb),
        grid=(b, nb),
        in_specs=[
            pl.BlockSpec(memory_space=pltpu.SMEM),
            pl.BlockSpec((None, blk, ATT_Q_DIM), lambda bi, j: (bi, j, 0)),
            pl.BlockSpec((None, n_ctx, kvw), lambda bi, j: (bi, 0, 0)),
            pl.BlockSpec((None, blk, kvw), lambda bi, j: (bi, jnp.maximum(j - 1, 0), 0)),
            pl.BlockSpec((None, blk, kvw), lambda bi, j: (bi, j, 0)),
            pl.BlockSpec((None, blk, kvw), lambda bi, j: (bi, jnp.minimum(j + 1, nb - 1), 0)),
        ],
        out_specs=pl.BlockSpec((None, blk, ATT_Q_DIM), lambda bi, j: (bi, j, 0)),
        out_shape=jax.ShapeDtypeStruct((b, s, ATT_Q_DIM), BF16),
        compiler_params=_cparams(("parallel", "parallel")),
        name="window_attention",
    )(sink, q, kv, kv, kv, kv)


def _rwkv_tile(d, i, n_tiles):
    return jnp.where(d == 0, i, jnp.where(i == 0, 0, n_tiles - i))


def _rwkv_ops_kernel(has_vres, emit_v, n_ctx_tiles, n_tiles, *refs):
    it = iter(refs)
    rw_ref, rwp_ref, rwn_ref = next(it), next(it), next(it)
    vf_ref = hd_ref = vup_ref = vb_ref = None
    if has_vres:
        vf_ref, hd_ref = next(it), next(it)
    conv_ref, wa_ref, w0a0_ref, kkw_ref, kaw_ref, rkw_ref, gup_ref = (next(it) for _ in range(7))
    if has_vres:
        vup_ref, vb_ref = next(it), next(it)
    gsum_ref, tri_ref, bdm_ref = next(it), next(it), next(it)
    rm_ref, yc_ref, aux_ref = next(it), next(it), next(it)
    v_out_ref = next(it) if emit_v else None
    fa_ref, fr_ref, fb_ref, fk_ref, fbh_ref, fkh_ref, fv_ref, fel_ref = (next(it) for _ in range(8))

    t = pl.program_id(1)
    tm = rw_ref.shape[0]
    c_dim = RWKV_DIM
    lc = RWKV_CHUNK
    n_chunks = tm // lc
    gw = RWKV_PACK * RWKV_HEAD
    n_groups = c_dim // gw

    x = rw_ref[...]
    ok_prev = jnp.logical_and(t != 0, t != n_ctx_tiles)
    ok_next = jnp.logical_and(t != n_ctx_tiles - 1, t != n_tiles - 1)
    halo_p = jnp.where(ok_prev, rwp_ref[7:8, :], 0.0)
    halo_n = jnp.where(ok_next, rwn_ref[0:1, :], 0.0)
    row8 = lax.broadcasted_iota(jnp.int32, (8, x.shape[1]), 0)
    xp = pltpu.roll(x, 1, 0)
    xp = jnp.concatenate([jnp.where(row8 == 0, halo_p, xp[0:8]), xp[8:]], axis=0)
    xn = pltpu.roll(x, tm - 1, 0)
    xn = jnp.concatenate([xn[0:tm - 8], jnp.where(row8 == 7, halo_n, xn[tm - 8:])], axis=0)
    cw = conv_ref[...]
    cv = (xp * cw[0:1] + x * cw[1:2]) + xn * cw[2:3]
    r = cv[:, 0:c_dim]
    k = cv[:, c_dim:2 * c_dim]
    v = cv[:, 2 * c_dim:3 * c_dim]
    xwa = cv[:, 3 * c_dim:3 * c_dim + LANES]
    xg = cv[:, 3 * c_dim + LANES:3 * c_dim + 2 * LANES]

    if emit_v:
        v_out_ref[...] = v
    if has_vres:
        gate = jax.nn.sigmoid(vb_ref[...] + _dot(hd_ref[...], vup_ref[...]))
        v = v + (vf_ref[...] - v) * gate
    fv_ref[...] = v

    gsum = gsum_ref[...]
    kk = k * kkw_ref[...]
    kk = kk * lax.rsqrt(_head_sums(kk * kk, gsum) + 1e-12)
    lane = lax.broadcasted_iota(jnp.int32, xwa.shape, 1)
    lora_in = jnp.where(lane < DECAY_LORA, jnp.tanh(xwa), xwa).astype(BF16)
    kd_sum = None
    for d in range(2):
        pre = jnp.dot(lora_in, wa_ref[d], preferred_element_type=F32) + w0a0_ref[d]
        a_sig = jax.nn.sigmoid(pre[:, c_dim:2 * c_dim])
        lw = -jnp.exp(-jax.nn.softplus(-pre[:, 0:c_dim]) - 0.5)
        kd = k * (1.0 + (a_sig - 1.0) * kaw_ref[...])
        kd_sum = kd if d == 0 else kd_sum + kd
        bvec = kk * a_sig
        hi, lo = _split2(lw)
        cs = jnp.dot(tri_ref[d], hi, preferred_element_type=F32) + jnp.dot(tri_ref[d], lo, preferred_element_type=F32)
        c_inc = cs[0:tm]
        c_tot = cs[tm:2 * tm]
        e_inc = jnp.exp(c_inc)
        e_neg = jnp.exp(-c_inc)
        e_rem = jnp.exp(c_tot - c_inc)
        fa_ref[d] = -kk * jnp.exp(c_inc - lw)
        fr_ref[d] = r * e_inc
        fb_ref[d] = bvec * e_neg
        fk_ref[d] = kd * e_neg
        fbh_ref[d] = bvec * e_rem
        fkh_ref[d] = kd * e_rem
        for c in range(n_chunks):
            fel_ref[d, c:c + 1, :] = jnp.exp(c_tot[c * lc:c * lc + 1, :])
    aux_ref[:, 0:c_dim] = _head_sums(r * kd_sum * rkw_ref[...], gsum) * v
    aux_ref[:, c_dim:2 * c_dim] = _dot(jax.nn.sigmoid(xg), gup_ref[...])

    bdm = bdm_ref[...]

    def bd(xb):
        return jnp.concatenate([xb] * RWKV_PACK, axis=0) * bdm

    low_half = lax.broadcasted_iota(jnp.int32, (lc, 2 * lc), 1) < lc

    def head_transpose(xf):
        w = jnp.concatenate([xf, xf], axis=0).T
        pairs = [jnp.where(low_half, w[(2 * j) * lc:(2 * j + 1) * lc], w[(2 * j + 1) * lc:(2 * j + 2) * lc])
                 for j in range(RWKV_PACK // 2)]
        return jnp.concatenate(pairs, axis=1)

    mm = lambda p, q: jnp.dot(p, q, preferred_element_type=F32)
    ri = lax.broadcasted_iota(jnp.int32, (2 * lc, 2 * gw), 0)
    ci = lax.broadcasted_iota(jnp.int32, (2 * lc, 2 * gw), 1) % lc
    amask = [jnp.logical_or(jnp.logical_and(ri < lc, sgn * (ri - ci) > 0),
                            jnp.logical_and(ri >= lc, sgn * (ri - lc - ci) >= 0)) for sgn in (1, -1)]
    eye_p = (lax.broadcasted_iota(jnp.int32, (lc, gw), 0)
             == lax.broadcasted_iota(jnp.int32, (lc, gw), 1) % lc).astype(F32)
    n_levels = int(math.log2(lc))

    probs = [(d, c, g) for d in range(2) for c in range(n_chunks) for g in range(n_groups)]
    blk = lambda ref, p: ref[p[0], p[1] * lc:(p[1] + 1) * lc, p[2] * gw:(p[2] + 1) * gw]
    vblk = lambda p: fv_ref[p[1] * lc:(p[1] + 1) * lc, p[2] * gw:(p[2] + 1) * gw]
    ab = [blk(fa_ref, p).astype(BF16) for p in probs]
    vb = [vblk(p).astype(BF16) for p in probs]
    a_all = []
    for j, p in enumerate(probs):
        lhs = jnp.concatenate([ab[j], blk(fr_ref, p).astype(BF16)], axis=0)
        rhs_t = jnp.concatenate([bd(blk(fb_ref, p).astype(BF16)), bd(blk(fk_ref, p).astype(BF16))], axis=0)
        full = lax.dot_general(lhs, rhs_t, (((1,), (1,)), ((), ())), preferred_element_type=F32)
        a_all.append(jnp.where(amask[p[0]], full, 0.0))
    a_rb = [m[lc:2 * lc, 0:gw].astype(BF16) for m in a_all]
    a_rk = [m[lc:2 * lc, gw:2 * gw].astype(BF16) for m in a_all]
    t_p = [eye_p + m[0:lc, 0:gw] for m in a_all]
    pb = [m[0:lc, 0:gw].astype(BF16) for m in a_all]
    pb = [mm(p, bd(p)).astype(BF16) for p in pb]
    for lvl in range(1, n_levels):
        if lvl < n_levels - 1:
            res = [mm(jnp.concatenate([p, t.astype(BF16)], axis=0), bd(p)) for p, t in zip(pb, t_p)]
            pb = [m[0:lc].astype(BF16) for m in res]
            t_p = [t + m[lc:2 * lc] for t, m in zip(t_p, res)]
        else:
            t_p = [t + mm(t.astype(BF16), bd(p)) for p, t in zip(pb, t_p)]
    u = [mm(m[0:lc, gw:2 * gw].astype(BF16), bd(vv)) for m, vv in zip(a_all, vb)]
    xs = [mm(t.astype(BF16), jnp.concatenate([bd(a), bd(uu.astype(BF16))], axis=1))
          for t, a, uu in zip(t_p, ab, u)]
    a1b = [m[:, 0:gw].astype(BF16) for m in xs]
    u1b = [m[:, gw:2 * gw].astype(BF16) for m in xs]
    for j, p in enumerate(probs):
        d, c, g = p
        zb = head_transpose(blk(fbh_ref, p)).astype(BF16)
        zk = head_transpose(blk(fkh_ref, p)).astype(BF16)
        res = mm(jnp.concatenate([a_rb[j], zb], axis=0), bd(a1b[j]))
        el_c = fel_ref[d, c:c + 1, g * gw:(g + 1) * gw]
        rm_ref[d, c, g] = jnp.concatenate([blk(fr_ref, p) + res[0:lc], res[lc:2 * lc] + eye_p * el_c],
                                          axis=0).astype(BF16)
        yc_ref[d, c, g] = mm(jnp.concatenate([jnp.concatenate([a_rb[j], a_rk[j]], axis=1),
                                              jnp.concatenate([zb, zk], axis=1)], axis=0),
                             jnp.concatenate([bd(u1b[j]), bd(vb[j])], axis=0))


def _rwkv_carry_kernel(n_tiles, rm_ref, yc_ref, bdm_ref, y_ref, st_ref):
    d = pl.program_id(0)
    i = pl.program_id(1)
    nb, n_chunks, n_groups, lc2, gw = rm_ref.shape
    lc = lc2 // 2
    bdm = bdm_ref[...]

    @pl.when(i == 0)
    def _():
        st_ref[...] = jnp.zeros_like(st_ref)

    for n in range(n_chunks):
        cc = jnp.where(d == 0, n, n_chunks - 1 - n)
        rows = pl.ds(pl.multiple_of(cc * lc, lc), lc)
        for bi in range(nb):
            for g in range(n_groups):
                st_bd = jnp.concatenate([st_ref[bi, g].astype(BF16)] * RWKV_PACK, axis=0) * bdm
                tot = jnp.dot(rm_ref[bi, cc, g], st_bd, preferred_element_type=F32) + yc_ref[bi, cc, g]
                y_ref[bi, rows, g * gw:(g + 1) * gw] = tot[0:lc]
                st_ref[bi, g] = tot[lc:2 * lc]


def _rwkv(rw, conv_w, wa, w0a0, k_k, k_a, r_k, g_up, gsum, tri, bdm, n_ctx_tiles, emit_v, vres=None):
    b, s, _ = rw.shape
    tm = TOKEN_TILE
    nt = s // tm
    c = RWKV_DIM
    lc = RWKV_CHUNK
    gw = RWKV_PACK * RWKV_HEAD
    n_groups, n_chunks = c // gw, tm // lc
    halo_blocks = tm // 8
    last_halo = s // 8 - 1
    tok = lambda width: pl.BlockSpec((None, tm, width), lambda bi, t: (bi, t, 0))
    full = lambda a: pl.BlockSpec(a.shape, lambda bi, t: (0,) * a.ndim)
    in_specs = [
        tok(RWKV_IN),
        pl.BlockSpec((None, 8, RWKV_IN), lambda bi, t: (bi, jnp.maximum(t * halo_blocks - 1, 0), 0)),
        pl.BlockSpec((None, 8, RWKV_IN), lambda bi, t: (bi, jnp.minimum((t + 1) * halo_blocks, last_halo), 0)),
    ]
    args = [rw, rw, rw]
    if vres is not None:
        v_first, hd, vup, vb = vres
        in_specs += [tok(c), tok(LANES)]
        args += [v_first, hd]
    in_specs += [full(conv_w), full(wa), full(w0a0), full(k_k), full(k_a), full(r_k), full(g_up)]
    args += [conv_w, wa, w0a0, k_k, k_a, r_k, g_up]
    if vres is not None:
        in_specs += [full(vup), full(vb)]
        args += [vup, vb]
    in_specs += [full(gsum), full(tri), full(bdm)]
    args += [gsum, tri, bdm]
    op_shape = (2, b, nt, n_chunks, n_groups, 2 * lc, gw)
    op_spec = pl.BlockSpec((2, None, None, n_chunks, n_groups, 2 * lc, gw), lambda bi, t: (0, bi, t, 0, 0, 0, 0))
    out_specs = [op_spec, op_spec, tok(2 * c)]
    out_shape = [jax.ShapeDtypeStruct(op_shape, BF16), jax.ShapeDtypeStruct(op_shape, F32),
                 jax.ShapeDtypeStruct((b, s, 2 * c), F32)]
    if emit_v:
        out_specs.append(tok(c))
        out_shape.append(jax.ShapeDtypeStruct((b, s, c), F32))
    feat_buf = pltpu.VMEM((2, tm, c), F32)
    scratch = [feat_buf] * 6 + [pltpu.VMEM((tm, c), F32), pltpu.VMEM((2, 8 * (-(-n_chunks // 8)), c), F32)]
    outs = pl.pallas_call(
        functools.partial(_rwkv_ops_kernel, vres is not None, emit_v, n_ctx_tiles, nt),
        grid=(b, nt),
        in_specs=in_specs,
        out_specs=out_specs,
        out_shape=out_shape,
        scratch_shapes=scratch,
        compiler_params=_cparams(("parallel", "parallel")),
        name="rwkv7_chunk_ops",
    )(*args)
    rm, yc = outs[0], outs[1]

    tile = lambda d, i: _rwkv_tile(d, i, nt)
    carry_in = pl.BlockSpec((None, b, None, n_chunks, n_groups, 2 * lc, gw), lambda d, i: (d, 0, tile(d, i), 0, 0, 0, 0))
    y = pl.pallas_call(
        functools.partial(_rwkv_carry_kernel, nt),
        grid=(2, nt),
        in_specs=[carry_in, carry_in, pl.BlockSpec(bdm.shape, lambda d, i: (0, 0))],
        out_specs=pl.BlockSpec((None, b, tm, c), lambda d, i: (d, 0, tile(d, i), 0)),
        out_shape=jax.ShapeDtypeStruct((2, b, s, c), F32),
        scratch_shapes=[pltpu.VMEM((b, n_groups, lc, gw), F32)],
        compiler_params=_cparams(("parallel", "arbitrary")),
        name="rwkv7_state_carry",
    )(rm, yc, bdm)
    return (y,) + tuple(outs[2:])


def _merge_kernel(x_ref, mod_ref, gain_ref, wgt_ref, oa_ref, y_ref, aux_ref, os_ref, lnw_ref, lnb_ref, gm_ref,
                  woa_ref, wor_ref, wos_ref, wout_ref, o_ref):
    d_model = x_ref.shape[-1]
    c = RWKV_DIM
    x = x_ref[...]
    mod = mod_ref[...]
    hb = _rms_mod(x, gain_ref[...], mod[0:1], mod[1:2]).astype(BF16)
    gm = gm_ref[...]
    gates, o_rwkv = [], []

    def gate_stream():
        for j in range(N_BRANCH):
            gates.append(jax.nn.sigmoid(jnp.dot(hb, wgt_ref[:, j * d_model:(j + 1) * d_model],
                                                preferred_element_type=F32)))
            yield

    def rwkv_epilogue():
        y = y_ref[0] + y_ref[1]
        mu = _head_sums(y, gm)
        yield
        yc = y - mu
        var = _head_sums(yc * yc, gm)
        yield
        yn = yc * lax.rsqrt(var + GN_EPS) * lnw_ref[...] + lnb_ref[...]
        o_rwkv.append(((yn + aux_ref[:, 0:c]) * aux_ref[:, c:2 * c]).astype(BF16))
        yield

    _alternate(gate_stream(), rwkv_epilogue())
    m = (gates[0] * jnp.dot(oa_ref[...], woa_ref[...], preferred_element_type=F32)
         + gates[1] * jnp.dot(o_rwkv[0], wor_ref[...], preferred_element_type=F32)
         + gates[2] * jnp.dot(os_ref[...], wos_ref[...], preferred_element_type=F32))
    mix = _dot(m, wout_ref[...])
    o_ref[...] = x + mod[2:3] * mix


def _merge(xs, modt, gain, w_gt, o_attn, y, aux, o_sgu, ln_w, ln_b, gmean, woa, wor, wos, wout, n_ctx_tiles):
    b, s, d = xs.shape
    tm = TOKEN_TILE
    tok = lambda width: pl.BlockSpec((None, tm, width), lambda bi, i: (bi, i, 0))
    tok2 = lambda width: pl.BlockSpec((2, None, tm, width), lambda bi, i: (0, bi, i, 0))
    full = lambda a: pl.BlockSpec(a.shape, lambda bi, i: (0,) * a.ndim)
    return pl.pallas_call(
        _merge_kernel,
        grid=(b, s // tm),
        in_specs=[
            tok(d),
            pl.BlockSpec((None, None, 6, d), lambda bi, i: (bi, jnp.where(i < n_ctx_tiles, 0, 1), 0, 0)),
            full(gain), full(w_gt),
            tok(ATT_Q_DIM), tok2(RWKV_DIM), tok(2 * RWKV_DIM), tok(SGU_DIM),
            full(ln_w), full(ln_b), full(gmean), full(woa), full(wor), full(wos), full(wout),
        ],
        out_specs=tok(d),
        out_shape=jax.ShapeDtypeStruct((b, s, d), F32),
        compiler_params=_cparams(("parallel", "parallel")),
        name="merge_out",
    )(xs, modt, gain, w_gt, o_attn, y, aux, o_sgu, ln_w, ln_b, gmean, woa, wor, wos, wout)


def _ffn_kernel(x_ref, mod_ref, gain_ref, w1_ref, w3_ref, w2_ref, o_ref):
    x = x_ref[...]
    mod = mod_ref[...]
    hb = _rms_mod(x, gain_ref[...], mod[3:4], mod[4:5]).astype(BF16)
    a = jnp.dot(hb, w1_ref[...], preferred_element_type=F32)
    bq = jnp.dot(hb, w3_ref[...], preferred_element_type=F32)
    hid = (a * jax.nn.sigmoid(a)) * bq
    o_ref[...] = x + mod[5:6] * _dot(hid, w2_ref[...])


def _ffn(xs, modt, gain, w1, w3, w2, n_ctx_tiles, first_tile):
    b, s, d = xs.shape
    tm = TOKEN_TILE
    nt = s // tm - first_tile
    full = lambda a: pl.BlockSpec(a.shape, lambda bi, i: (0,) * a.ndim)
    return pl.pallas_call(
        _ffn_kernel,
        grid=(b, nt),
        in_specs=[
            pl.BlockSpec((None, tm, d), lambda bi, i: (bi, i + first_tile, 0)),
            pl.BlockSpec((None, None, 6, d), lambda bi, i: (bi, jnp.where(i + first_tile < n_ctx_tiles, 0, 1), 0, 0)),
            full(gain), full(w1), full(w3), full(w2),
        ],
        out_specs=pl.BlockSpec((None, tm, d), lambda bi, i: (bi, i, 0)),
        out_shape=jax.ShapeDtypeStruct((b, nt * tm, d), F32),
        compiler_params=_cparams(("parallel", "parallel")),
        name="swiglu_ffn",
    )(xs, modt, gain, w1, w3, w2)


def _rope_tables(n_ctx, n_lat):
    rows = n_lat // GRID_W
    row = jnp.repeat(jnp.arange(rows), GRID_W).astype(F32)
    col = jnp.tile(jnp.arange(GRID_W), rows).astype(F32)
    inv = ROPE_BASE ** (-jnp.arange(ROPE_FREQS, dtype=F32) / ROPE_FREQS)
    ang = jnp.concatenate([row[:, None] * inv, col[:, None] * inv], -1)
    cos, sin = jnp.cos(ang), jnp.sin(ang)
    cos = jnp.concatenate([jnp.ones((n_ctx, HEAD_DIM // 2), F32), cos], 0)
    sin = jnp.concatenate([jnp.zeros((n_ctx, HEAD_DIM // 2), F32), sin], 0)
    return jnp.tile(jnp.concatenate([cos, cos], -1), (1, 2)), jnp.tile(jnp.concatenate([-sin, sin], -1), (1, 2))


def _block_diag_ones(n, blk, scale):
    i = np.arange(n)
    return jnp.asarray(((i[:, None] // blk) == (i[None, :] // blk)).astype(np.float32) * scale, dtype=BF16)


def _cumsum_matrices(tm, lc):
    i = np.arange(tm)
    same = (i[:, None] // lc) == (i[None, :] // lc)
    fwd = same & (i[None, :] <= i[:, None])
    bwd = same & (i[None, :] >= i[:, None])
    mats = np.stack([np.concatenate([fwd, same], 0), np.concatenate([bwd, same], 0)]).astype(np.float32)
    return jnp.asarray(mats, dtype=BF16)


def kernel(x, c, ctx, c_ctx, w_mod, b_mod, norm_mix, norm_ffn, w_in, q_gain, k_gain, attn_sink,
           rwkv_conv, rwkv_w0, rwkv_w_up, rwkv_a0, rwkv_a_up, rwkv_k_k, rwkv_k_a, rwkv_r_k, rwkv_g_up,
           rwkv_ln_w, rwkv_ln_b, rwkv_vres_down, rwkv_vres_up, rwkv_vres_b,
           sgu_ln_w, sgu_ln_b, sgu_w, sgu_b, w_o_attn, w_o_rwkv, w_o_sgu, w_out,
           ffn_w1, ffn_w3, ffn_w2):
    b, n_lat, d = x.shape
    n_ctx = ctx.shape[1]
    depth = w_in.shape[0]
    tm = TOKEN_TILE
    assert n_ctx % tm == 0 and n_lat % tm == 0 and n_lat % GRID_W == 0 and 2 * RWKV_CHUNK == LANES
    n_ctx_tiles = n_ctx // tm
    cdim = RWKV_DIM

    xs = jnp.concatenate([ctx, x], axis=1)
    cos2, sin2 = _rope_tables(n_ctx, n_lat)
    gmean = _block_diag_ones(MXU_DIM, HEAD_DIM, 1.0 / HEAD_DIM)
    gsum = _block_diag_ones(MXU_DIM, RWKV_HEAD, 1.0)
    tri = _cumsum_matrices(tm, RWKV_CHUNK)
    bdm = _block_diag_ones(RWKV_PACK * RWKV_HEAD, RWKV_HEAD, 1.0)

    rows = -(-(b + 1) // 8) * 8
    cc = jnp.zeros((rows, d), F32).at[:b].set(c).at[b].set(c_ctx)
    mods = _modulation(cc, w_mod, b_mod)

    row2 = lambda a: a.reshape(1, -1)
    v_first = None
    for l in range(depth):
        last = l == depth - 1
        has_vres = l > 0
        mod_lat = mods[l, :b].reshape(b, 1, 6, d)
        mod_ctx = jnp.broadcast_to(mods[l, b].reshape(1, 1, 6, d), (b, 1, 6, d))
        modt = jnp.concatenate([mod_ctx, mod_lat], axis=1)

        n_proj = w_in.shape[2] - N_BRANCH * d
        w_ext = w_in[l, :, :n_proj]
        if has_vres:
            down = jnp.pad(rwkv_vres_down[l - 1], ((0, 0), (0, LANES - VRES_LORA)))
            w_ext = jnp.concatenate([w_ext, down], axis=1)
        w_ext = w_ext.astype(BF16)
        w_gt = w_in[l, :, n_proj:].astype(BF16)
        qg2 = jnp.tile(q_gain[l], 2).reshape(1, LANES)
        kg2 = jnp.tile(k_gain[l], 2).reshape(1, LANES)
        sgu_bb = jnp.broadcast_to(sgu_b[l][:, :, None], (SGU_GROUPS, SGU_CHUNK, SGU_DIM // SGU_GROUPS))
        outs = _proj_in(xs, modt, row2(norm_mix[l]), w_ext, cos2, sin2, qg2, kg2, gmean,
                        row2(sgu_ln_w[l]), row2(sgu_ln_b[l]), sgu_w[l].astype(BF16), sgu_bb, n_ctx_tiles, has_vres)
        q, kv, rw, o_sgu = outs[:4]

        o_attn = _attention(q, kv, attn_sink[l], n_ctx)

        zeros_lora = jnp.zeros((2, DECAY_LORA, cdim), F32)
        wa = jnp.concatenate([jnp.concatenate([rwkv_w_up[l], zeros_lora], axis=2),
                              jnp.concatenate([zeros_lora, rwkv_a_up[l]], axis=2)], axis=1).astype(BF16)
        w0a0 = jnp.concatenate([rwkv_w0[l], rwkv_a0[l]], axis=1).reshape(2, 1, 2 * cdim)
        vres = None
        if has_vres:
            vup = jnp.pad(rwkv_vres_up[l - 1], ((0, LANES - VRES_LORA), (0, 0))).astype(BF16)
            vres = (v_first, outs[4], vup, row2(rwkv_vres_b[l - 1]))
        r_outs = _rwkv(rw, rwkv_conv[l], wa, w0a0, row2(rwkv_k_k[l]), row2(rwkv_k_a[l]), row2(rwkv_r_k[l]),
                       rwkv_g_up[l].astype(BF16), gsum, tri, bdm, n_ctx_tiles, emit_v=(l == 0), vres=vres)
        y, aux = r_outs[0], r_outs[1]
        if l == 0:
            v_first = r_outs[2]

        xs = _merge(xs, modt, row2(norm_mix[l]), w_gt, o_attn, y, aux, o_sgu, row2(rwkv_ln_w[l]), row2(rwkv_ln_b[l]),
                    gmean, w_o_attn[l].astype(BF16), w_o_rwkv[l].astype(BF16), w_o_sgu[l].astype(BF16),
                    w_out[l].astype(BF16), n_ctx_tiles)
        xs = _ffn(xs, modt, row2(norm_ffn[l]), ffn_w1[l].astype(BF16), ffn_w3[l].astype(BF16),
                  ffn_w2[l].astype(BF16), n_ctx_tiles, first_tile=n_ctx_tiles if last else 0)
    return xs
```

```python
import functools
import math

import jax
import jax.numpy as jnp
import numpy as np
from jax import lax
from jax.experimental import pallas as pl
from jax.experimental.pallas import tpu as pltpu

HEAD_DIM = 64
ATT_HEADS = 8
ATT_KV_HEADS = 2
GQA_GROUP = ATT_HEADS // ATT_KV_HEADS
ATT_Q_DIM = ATT_HEADS * HEAD_DIM
ATT_KV_DIM = ATT_KV_HEADS * HEAD_DIM
ATT_BLOCK = 128
ATT_QBLOCKS = 2
ATT_SCALE = HEAD_DIM ** -0.5
ROPE_BASE = 10000.0
ROPE_FREQS = HEAD_DIM // 4
GRID_W = 64
RWKV_HEADS = 8
RWKV_HEAD = 64
RWKV_DIM = RWKV_HEADS * RWKV_HEAD
DECAY_LORA = 64
ICLR_LORA = 64
VRES_LORA = 32
GATE_LORA = 128
RWKV_IN = 3 * RWKV_DIM + DECAY_LORA + ICLR_LORA + GATE_LORA
GN_EPS = 64e-5
SGU_CHUNK = 128
SGU_GROUPS = 4
SGU_DIM = 512
N_BRANCH = 3
EPS = 1e-6

LANES = 128
TOKEN_TILE = 256
RWKV_CHUNK = 64
MXU_DIM = 256
RWKV_PACK = MXU_DIM // RWKV_HEAD
VMEM_LIMIT = 56 * 1024 * 1024

BF16 = jnp.bfloat16
F32 = jnp.float32


def _cparams(sem):
    return pltpu.CompilerParams(dimension_semantics=sem, vmem_limit_bytes=VMEM_LIMIT)


def _dot(a, b):
    return jnp.dot(a.astype(BF16), b.astype(BF16), preferred_element_type=F32)


def _split2(x):
    hi = x.astype(BF16)
    lo = (x - hi.astype(F32)).astype(BF16)
    return hi, lo


def _head_sums(x, g_bf16):
    w = g_bf16.shape[0]
    hi, lo = _split2(x)
    parts = [jnp.dot(hi[:, j:j + w], g_bf16, preferred_element_type=F32)
             + jnp.dot(lo[:, j:j + w], g_bf16, preferred_element_type=F32) for j in range(0, x.shape[1], w)]
    return parts[0] if len(parts) == 1 else jnp.concatenate(parts, axis=1)


def _alternate(*gens):
    live = list(gens)
    while live:
        for g in list(live):
            try:
                next(g)
            except StopIteration:
                live.remove(g)


def _rms_mod(x, gain, shift, scale):
    ms = jnp.mean(x * x, axis=-1, keepdims=True)
    return (x * lax.rsqrt(ms + EPS) * gain) * (1.0 + scale) + shift


def _mod_kernel(c_ref, w_ref, b_ref, o_ref):
    c = c_ref[...]
    o_ref[...] = _dot(c * jax.nn.sigmoid(c), w_ref[...]) + b_ref[...]


def _modulation(cc, w_mod, b_mod):
    nl, d, n6 = w_mod.shape
    rows = cc.shape[0]
    tn = 1536
    return pl.pallas_call(
        _mod_kernel,
        grid=(nl, n6 // tn),
        in_specs=[
            pl.BlockSpec((rows, d), lambda l, j: (0, 0)),
            pl.BlockSpec((None, d, tn), lambda l, j: (l, 0, j)),
            pl.BlockSpec((None, 1, tn), lambda l, j: (l, 0, j)),
        ],
        out_specs=pl.BlockSpec((None, rows, tn), lambda l, j: (l, 0, j)),
        out_shape=jax.ShapeDtypeStruct((nl, rows, n6), F32),
        compiler_params=_cparams(("parallel", "parallel")),
        name="modulation",
    )(cc, w_mod, b_mod.reshape(nl, 1, n6))


def _stream_tile(n_ctx_tiles, x_refs):
    if len(x_refs) == 1:
        return x_refs[0][...]
    return jnp.where(pl.program_id(1) < n_ctx_tiles, x_refs[0][...], x_refs[1][...])


def _stream_specs(parts, tm, n_ctx_tiles):
    d = parts[0].shape[-1]
    if len(parts) == 1:
        return [pl.BlockSpec((None, tm, d), lambda bi, i: (bi, i, 0))]
    return [pl.BlockSpec((None, tm, d), lambda bi, i: (bi, jnp.minimum(i, n_ctx_tiles - 1), 0)),
            pl.BlockSpec((None, tm, d), lambda bi, i: (bi, jnp.maximum(i - n_ctx_tiles, 0), 0))]


def _proj_in_kernel(has_vres, n_ctx_tiles, n_parts, *refs):
    x_refs, refs = refs[:n_parts], refs[n_parts:]
    (mod_ref, gain_ref, w_ref, cos_ref, sin_ref, qg_ref, kg_ref, gm_ref, lnw_ref, lnb_ref, sw_ref, sb_ref,
     q_ref, kv_ref, rw_ref, os_ref, *maybe_hd_ref) = refs
    x = _stream_tile(n_ctx_tiles, x_refs)
    mod = mod_ref[...]
    hb = _rms_mod(x, gain_ref[...], mod[0:1], mod[1:2]).astype(BF16)
    cos = cos_ref[...]
    sin = sin_ref[...]
    gm = gm_ref[...]

    def norm_rope(p, gain):
        reps = p.shape[1] // LANES
        tile = lambda a: a if reps == 1 else jnp.concatenate([a] * reps, axis=1)
        lane = lax.broadcasted_iota(jnp.int32, p.shape, 1)
        first_half = (lane % HEAD_DIM) < (HEAD_DIM // 2)
        ms = _head_sums(p * p, gm[:p.shape[1], :p.shape[1]])
        t = p * lax.rsqrt(ms + EPS) * tile(gain)
        swapped = jnp.where(first_half, pltpu.roll(t, p.shape[1] - HEAD_DIM // 2, 1), pltpu.roll(t, HEAD_DIM // 2, 1))
        return t * tile(cos) + swapped * tile(sin)

    proj = lambda c0, width: jnp.dot(hb, w_ref[:, c0:c0 + width], preferred_element_type=F32)
    c_q, c_kv = 0, ATT_Q_DIM
    c_rw = c_kv + 2 * ATT_KV_DIM
    c_sg = c_rw + RWKV_IN
    c_hd = c_sg + 2 * SGU_DIM
    gd = SGU_DIM // SGU_GROUPS

    def rwkv_columns():
        for j in range(0, RWKV_IN, MXU_DIM):
            rw_ref[:, j:j + MXU_DIM] = proj(c_rw + j, MXU_DIM)
            yield
        if has_vres:
            maybe_hd_ref[0][...] = proj(c_hd, LANES)
        yield

    def epilogues():
        pq = [proj(c_q + j, MXU_DIM) for j in range(0, ATT_Q_DIM, MXU_DIM)]
        pkv = proj(c_kv, 2 * ATT_KV_DIM)
        yield
        pu = proj(c_sg, SGU_DIM)
        for j, p in enumerate(pq):
            q_ref[:, j * MXU_DIM:(j + 1) * MXU_DIM] = norm_rope(p, qg_ref[...]).astype(BF16)
            yield
        pvg = proj(c_sg + SGU_DIM, SGU_DIM)
        kv_ref[:, 0:ATT_KV_DIM] = norm_rope(pkv[:, 0:ATT_KV_DIM], kg_ref[...]).astype(BF16)
        kv_ref[:, ATT_KV_DIM:2 * ATT_KV_DIM] = pkv[:, ATT_KV_DIM:2 * ATT_KV_DIM].astype(BF16)
        yield
        u = jax.nn.gelu(pu)
        yield
        vg = jax.nn.gelu(pvg)
        yield
        mu = jnp.mean(vg, axis=-1, keepdims=True)
        var = jnp.mean(jnp.square(vg - mu), axis=-1, keepdims=True)
        vn = ((vg - mu) * lax.rsqrt(var + EPS) * lnw_ref[...] + lnb_ref[...]).astype(BF16)
        yield
        for ck in range(x.shape[0] // SGU_CHUNK):
            rows = slice(ck * SGU_CHUNK, (ck + 1) * SGU_CHUNK)
            for g in range(SGU_GROUPS):
                cols = slice(g * gd, (g + 1) * gd)
                sp = jnp.dot(sw_ref[g], vn[rows, cols], preferred_element_type=F32) + sb_ref[g]
                os_ref[rows, cols] = (u[rows, cols] * sp).astype(BF16)
            yield

    _alternate(epilogues(), rwkv_columns())


def _proj_in(parts, modt, gain, w_ext, cos2, sin2, qg, kg, gmean, ln_w, ln_b, sgu_w, sgu_bb, n_ctx_tiles, has_vres):
    b, d = parts[0].shape[0], parts[0].shape[2]
    s = sum(p.shape[1] for p in parts)
    tm = TOKEN_TILE
    nt = s // tm
    tok = lambda width: pl.BlockSpec((None, tm, width), lambda bi, i: (bi, i, 0))
    full = lambda a: pl.BlockSpec(a.shape, lambda bi, i: (0,) * a.ndim)
    out_shapes = [
        jax.ShapeDtypeStruct((b, s, ATT_Q_DIM), BF16),
        jax.ShapeDtypeStruct((b, s, 2 * ATT_KV_DIM), BF16),
        jax.ShapeDtypeStruct((b, s, RWKV_IN), F32),
        jax.ShapeDtypeStruct((b, s, SGU_DIM), BF16),
    ]
    out_specs = [tok(ATT_Q_DIM), tok(2 * ATT_KV_DIM), tok(RWKV_IN), tok(SGU_DIM)]
    if has_vres:
        out_shapes.append(jax.ShapeDtypeStruct((b, s, LANES), F32))
        out_specs.append(tok(LANES))
    return pl.pallas_call(
        functools.partial(_proj_in_kernel, has_vres, n_ctx_tiles, len(parts)),
        grid=(b, nt),
        in_specs=_stream_specs(parts, tm, n_ctx_tiles) + [
            pl.BlockSpec((None, None, 6, d), lambda bi, i: (bi, jnp.where(i < n_ctx_tiles, 0, 1), 0, 0)),
            full(gain), full(w_ext),
            pl.BlockSpec((tm, LANES), lambda bi, i: (i, 0)),
            pl.BlockSpec((tm, LANES), lambda bi, i: (i, 0)),
            full(qg), full(kg), full(gmean), full(ln_w), full(ln_b), full(sgu_w), full(sgu_bb),
        ],
        out_specs=out_specs,
        out_shape=out_shapes,
        compiler_params=_cparams(("parallel", "parallel")),
        name="proj_in",
    )(*parts, modt, gain, w_ext, cos2, sin2, qg, kg, gmean, ln_w, ln_b, sgu_w, sgu_bb)


def _attn_kernel(n_ctx_blocks, n_blocks, sink_ref, q_ref, kvc_ref, kvp_ref, kvo_ref, kvn_ref, o_ref):
    blk = ATT_BLOCK
    rows = GQA_GROUP * blk
    nband = 3 * blk
    qq = lax.broadcasted_iota(jnp.int32, (blk, nband), 0)
    kcol = lax.broadcasted_iota(jnp.int32, (blk, nband), 1)
    kk = kcol % blk
    rgrp = lax.broadcasted_iota(jnp.int32, (rows, 1), 0) // blk
    kvc = kvc_ref[...]
    kv_blocks = [kvp_ref[...]] + [kvo_ref[i * blk:(i + 1) * blk, :] for i in range(ATT_QBLOCKS)] + [kvn_ref[...]]
    nt_dims = (((1,), (1,)), ((), ()))
    outs = [[None] * ATT_HEADS for _ in range(ATT_QBLOCKS)]

    def chain(qb, kh):
        j = pl.program_id(1) * ATT_QBLOCKS + qb
        is_lat = j >= n_ctx_blocks
        ok_prev = j >= n_ctx_blocks + 1
        ok_next = jnp.logical_and(is_lat, j <= n_blocks - 2)
        off_prev = jnp.where(ok_prev, 0, blk)
        off_next = jnp.where(ok_next, 0, blk)
        own_end = jnp.where(is_lat, 2 * blk, 0)
        band_mask = (((kcol < blk) & (kk >= qq + off_prev))
                     | ((kcol >= blk) & (kcol < own_end))
                     | ((kcol >= 2 * blk) & (kk <= qq - off_next)))
        band_bias = jnp.where(band_mask, 0.0, -jnp.inf)
        band_bias = jnp.concatenate([band_bias] * GQA_GROUP, axis=0)
        h0 = kh * GQA_GROUP
        q = q_ref[qb * blk:(qb + 1) * blk, :] * ATT_SCALE
        qh = jnp.concatenate([q[:, (h0 + g) * HEAD_DIM:(h0 + g + 1) * HEAD_DIM] for g in range(GQA_GROUP)], axis=0)
        ksl = slice(kh * HEAD_DIM, (kh + 1) * HEAD_DIM)
        vsl = slice(ATT_KV_DIM + kh * HEAD_DIM, ATT_KV_DIM + (kh + 1) * HEAD_DIM)
        band = kv_blocks[qb:qb + 3]
        k_band = jnp.concatenate([blkv[:, ksl] for blkv in band], axis=0)
        v_band = jnp.concatenate([blkv[:, vsl] for blkv in band], axis=0)
        s_ctx = lax.dot_general(qh, kvc[:, ksl], nt_dims, preferred_element_type=F32)
        s_band = lax.dot_general(qh, k_band, nt_dims, preferred_element_type=F32) + band_bias
        yield
        sink = jnp.zeros((rows, 1), F32)
        for g in range(GQA_GROUP):
            sink = jnp.where(rgrp == g, sink_ref[h0 + g], sink)
        m = jnp.maximum(jnp.maximum(jnp.max(s_ctx, axis=-1, keepdims=True), jnp.max(s_band, axis=-1, keepdims=True)), sink)
        yield
        p_ctx = jnp.exp(s_ctx - m)
        yield
        p_band = jnp.exp(s_band - m)
        yield
        denom = (jnp.sum(p_ctx, axis=-1, keepdims=True) + jnp.sum(p_band, axis=-1, keepdims=True)) + jnp.exp(sink - m)
        yield
        o = (jnp.dot(p_ctx.astype(BF16), kvc[:, vsl], preferred_element_type=F32)
             + jnp.dot(p_band.astype(BF16), v_band, preferred_element_type=F32)) / denom
        for g in range(GQA_GROUP):
            outs[qb][h0 + g] = o[g * blk:(g + 1) * blk]
        yield

    _alternate(*[chain(qb, kh) for qb in range(ATT_QBLOCKS) for kh in range(ATT_KV_HEADS)])
    for qb in range(ATT_QBLOCKS):
        o_ref[qb * blk:(qb + 1) * blk, :] = jnp.concatenate(outs[qb], axis=1).astype(BF16)


def _attention(q, kv, sink, n_ctx):
    b, s, _ = q.shape
    blk = ATT_BLOCK
    nb = s // blk
    ncb = n_ctx // blk
    kvw = kv.shape[-1]
    nq = ATT_QBLOCKS
    assert nb % nq == 0 and ncb % nq == 0
    return pl.pallas_call(
        functools.partial(_attn_kernel, ncb, nb),
        grid=(b, nb // nq),
        in_specs=[
            pl.BlockSpec(memory_space=pltpu.SMEM),
            pl.BlockSpec((None, nq * blk, ATT_Q_DIM), lambda bi, j: (bi, j, 0)),
            pl.BlockSpec((None, n_ctx, kvw), lambda bi, j: (bi, 0, 0)),
            pl.BlockSpec((None, blk, kvw), lambda bi, j: (bi, jnp.maximum(nq * j - 1, 0), 0)),
            pl.BlockSpec((None, nq * blk, kvw), lambda bi, j: (bi, j, 0)),
            pl.BlockSpec((None, blk, kvw), lambda bi, j: (bi, jnp.minimum(nq * j + nq, nb - 1), 0)),
        ],
        out_specs=pl.BlockSpec((None, nq * blk, ATT_Q_DIM), lambda bi, j: (bi, j, 0)),
        out_shape=jax.ShapeDtypeStruct((b, s, ATT_Q_DIM), BF16),
        compiler_params=_cparams(("parallel", "parallel")),
        name="window_attention",
    )(sink, q, kv, kv, kv, kv)


def _rwkv_tile(d, i, n_tiles):
    return jnp.where(d == 0, i, jnp.where(i == 0, 0, n_tiles - i))


def _rwkv_ops_kernel(has_vres, emit_v, n_ctx_tiles, n_tiles, *refs):
    it = iter(refs)
    rw_ref, rwp_ref, rwn_ref = next(it), next(it), next(it)
    vf_ref = hd_ref = vup_ref = vb_ref = None
    if has_vres:
        vf_ref, hd_ref = next(it), next(it)
    conv_ref, wa_ref, w0a0_ref, kkw_ref, kaw_ref, rkw_ref, gup_ref = (next(it) for _ in range(7))
    if has_vres:
        vup_ref, vb_ref = next(it), next(it)
    gsum_ref, tri_ref, bdm_ref = next(it), next(it), next(it)
    rm_ref, yc_ref, aux_ref = next(it), next(it), next(it)
    v_out_ref = next(it) if emit_v else None
    fa_ref, fr_ref, fb_ref, fk_ref, fbh_ref, fkh_ref, fv_ref, fel_ref = (next(it) for _ in range(8))

    t = pl.program_id(1)
    tm = rw_ref.shape[0]
    c_dim = RWKV_DIM
    lc = RWKV_CHUNK
    n_chunks = tm // lc
    gw = RWKV_PACK * RWKV_HEAD
    n_groups = c_dim // gw

    x = rw_ref[...]
    ok_prev = jnp.logical_and(t != 0, t != n_ctx_tiles)
    ok_next = jnp.logical_and(t != n_ctx_tiles - 1, t != n_tiles - 1)
    halo_p = jnp.where(ok_prev, rwp_ref[7:8, :], 0.0)
    halo_n = jnp.where(ok_next, rwn_ref[0:1, :], 0.0)
    row8 = lax.broadcasted_iota(jnp.int32, (8, x.shape[1]), 0)
    xp = pltpu.roll(x, 1, 0)
    xp = jnp.concatenate([jnp.where(row8 == 0, halo_p, xp[0:8]), xp[8:]], axis=0)
    xn = pltpu.roll(x, tm - 1, 0)
    xn = jnp.concatenate([xn[0:tm - 8], jnp.where(row8 == 7, halo_n, xn[tm - 8:])], axis=0)
    cw = conv_ref[...]
    cv = (xp * cw[0:1] + x * cw[1:2]) + xn * cw[2:3]
    r = cv[:, 0:c_dim]
    k = cv[:, c_dim:2 * c_dim]
    v = cv[:, 2 * c_dim:3 * c_dim]
    xwa = cv[:, 3 * c_dim:3 * c_dim + LANES]
    xg = cv[:, 3 * c_dim + LANES:3 * c_dim + 2 * LANES]

    if emit_v:
        v_out_ref[...] = v
    if has_vres:
        gate = jax.nn.sigmoid(vb_ref[...] + _dot(hd_ref[...], vup_ref[...]))
        v = v + (vf_ref[...] - v) * gate
    fv_ref[...] = v

    gsum = gsum_ref[...]
    kk = k * kkw_ref[...]
    kk = kk * lax.rsqrt(_head_sums(kk * kk, gsum) + 1e-12)
    lane = lax.broadcasted_iota(jnp.int32, xwa.shape, 1)
    lora_in = jnp.where(lane < DECAY_LORA, jnp.tanh(xwa), xwa).astype(BF16)
    kd_sum = None
    for d in range(2):
        pre = jnp.dot(lora_in, wa_ref[d], preferred_element_type=F32) + w0a0_ref[d]
        a_sig = jax.nn.sigmoid(pre[:, c_dim:2 * c_dim])
        lw = -jnp.exp(-jax.nn.softplus(-pre[:, 0:c_dim]) - 0.5)
        kd = k * (1.0 + (a_sig - 1.0) * kaw_ref[...])
        kd_sum = kd if d == 0 else kd_sum + kd
        bvec = kk * a_sig
        hi, lo = _split2(lw)
        cs = jnp.dot(tri_ref[d], hi, preferred_element_type=F32) + jnp.dot(tri_ref[d], lo, preferred_element_type=F32)
        c_inc = cs[0:tm]
        c_tot = cs[tm:2 * tm]
        e_inc = jnp.exp(c_inc)
        e_neg = jnp.exp(-c_inc)
        e_rem = jnp.exp(c_tot - c_inc)
        fa_ref[d] = -kk * jnp.exp(c_inc - lw)
        fr_ref[d] = r * e_inc
        fb_ref[d] = bvec * e_neg
        fk_ref[d] = kd * e_neg
        fbh_ref[d] = bvec * e_rem
        fkh_ref[d] = kd * e_rem
        for c in range(n_chunks):
            fel_ref[d, c:c + 1, :] = jnp.exp(c_tot[c * lc:c * lc + 1, :])
    aux_ref[:, 0:c_dim] = _head_sums(r * kd_sum * rkw_ref[...], gsum) * v
    aux_ref[:, c_dim:2 * c_dim] = _dot(jax.nn.sigmoid(xg), gup_ref[...])

    bdm = bdm_ref[...]

    def bd(xb):
        return jnp.concatenate([xb] * RWKV_PACK, axis=0) * bdm

    low_half = lax.broadcasted_iota(jnp.int32, (lc, 2 * lc), 1) < lc

    def head_transpose(xf):
        w = jnp.concatenate([xf, xf], axis=0).T
        pairs = [jnp.where(low_half, w[(2 * j) * lc:(2 * j + 1) * lc], w[(2 * j + 1) * lc:(2 * j + 2) * lc])
                 for j in range(RWKV_PACK // 2)]
        return jnp.concatenate(pairs, axis=1)

    mm = lambda p, q: jnp.dot(p, q, preferred_element_type=F32)
    ri = lax.broadcasted_iota(jnp.int32, (2 * lc, 2 * gw), 0)
    ci = lax.broadcasted_iota(jnp.int32, (2 * lc, 2 * gw), 1) % lc
    amask = [jnp.logical_or(jnp.logical_and(ri < lc, sgn * (ri - ci) > 0),
                            jnp.logical_and(ri >= lc, sgn * (ri - lc - ci) >= 0)) for sgn in (1, -1)]
    eye_p = (lax.broadcasted_iota(jnp.int32, (lc, gw), 0)
             == lax.broadcasted_iota(jnp.int32, (lc, gw), 1) % lc).astype(F32)
    n_levels = int(math.log2(lc))

    probs = [(d, c, g) for d in range(2) for c in range(n_chunks) for g in range(n_groups)]
    blk = lambda ref, p: ref[p[0], p[1] * lc:(p[1] + 1) * lc, p[2] * gw:(p[2] + 1) * gw]
    vblk = lambda p: fv_ref[p[1] * lc:(p[1] + 1) * lc, p[2] * gw:(p[2] + 1) * gw]
    ab = [blk(fa_ref, p).astype(BF16) for p in probs]
    vb = [vblk(p).astype(BF16) for p in probs]
    a_all = []
    for j, p in enumerate(probs):
        lhs = jnp.concatenate([ab[j], blk(fr_ref, p).astype(BF16)], axis=0)
        rhs_t = jnp.concatenate([bd(blk(fb_ref, p).astype(BF16)), bd(blk(fk_ref, p).astype(BF16))], axis=0)
        full = lax.dot_general(lhs, rhs_t, (((1,), (1,)), ((), ())), preferred_element_type=F32)
        a_all.append(jnp.where(amask[p[0]], full, 0.0))
    a_rb = [m[lc:2 * lc, 0:gw].astype(BF16) for m in a_all]
    a_rk = [m[lc:2 * lc, gw:2 * gw].astype(BF16) for m in a_all]
    t_p = [eye_p + m[0:lc, 0:gw] for m in a_all]
    pb = [m[0:lc, 0:gw].astype(BF16) for m in a_all]
    pb = [mm(p, bd(p)).astype(BF16) for p in pb]
    for lvl in range(1, n_levels):
        if lvl < n_levels - 1:
            res = [mm(jnp.concatenate([p, t.astype(BF16)], axis=0), bd(p)) for p, t in zip(pb, t_p)]
            pb = [m[0:lc].astype(BF16) for m in res]
            t_p = [t + m[lc:2 * lc] for t, m in zip(t_p, res)]
        else:
            t_p = [t + mm(t.astype(BF16), bd(p)) for p, t in zip(pb, t_p)]
    u = [mm(m[0:lc, gw:2 * gw].astype(BF16), bd(vv)) for m, vv in zip(a_all, vb)]
    xs = [mm(t.astype(BF16), jnp.concatenate([bd(a), bd(uu.astype(BF16))], axis=1))
          for t, a, uu in zip(t_p, ab, u)]
    a1b = [m[:, 0:gw].astype(BF16) for m in xs]
    u1b = [m[:, gw:2 * gw].astype(BF16) for m in xs]
    for j, p in enumerate(probs):
        d, c, g = p
        zb = head_transpose(blk(fbh_ref, p)).astype(BF16)
        zk = head_transpose(blk(fkh_ref, p)).astype(BF16)
        res = mm(jnp.concatenate([a_rb[j], zb], axis=0), bd(a1b[j]))
        el_c = fel_ref[d, c:c + 1, g * gw:(g + 1) * gw]
        rm_ref[d, c, g] = jnp.concatenate([blk(fr_ref, p) + res[0:lc], res[lc:2 * lc] + eye_p * el_c],
                                          axis=0).astype(BF16)
        yc_ref[d, c, g] = mm(jnp.concatenate([jnp.concatenate([a_rb[j], a_rk[j]], axis=1),
                                              jnp.concatenate([zb, zk], axis=1)], axis=0),
                             jnp.concatenate([bd(u1b[j]), bd(vb[j])], axis=0))


def _rwkv_carry_kernel(n_tiles, rm_ref, yc_ref, bdm_ref, y_ref, st_ref):
    d = pl.program_id(0)
    i = pl.program_id(1)
    nb, n_chunks, n_groups, lc2, gw = rm_ref.shape
    lc = lc2 // 2
    bdm = bdm_ref[...]

    @pl.when(i == 0)
    def _():
        st_ref[...] = jnp.zeros_like(st_ref)

    for n in range(n_chunks):
        cc = jnp.where(d == 0, n, n_chunks - 1 - n)
        rows = pl.ds(pl.multiple_of(cc * lc, lc), lc)
        for bi in range(nb):
            for g in range(n_groups):
                st_bd = jnp.concatenate([st_ref[bi, g].astype(BF16)] * RWKV_PACK, axis=0) * bdm
                tot = jnp.dot(rm_ref[bi, cc, g], st_bd, preferred_element_type=F32) + yc_ref[bi, cc, g]
                y_ref[bi, rows, g * gw:(g + 1) * gw] = tot[0:lc]
                st_ref[bi, g] = tot[lc:2 * lc]


def _rwkv(rw, conv_w, wa, w0a0, k_k, k_a, r_k, g_up, gsum, tri, bdm, n_ctx_tiles, emit_v, vres=None):
    b, s, _ = rw.shape
    tm = TOKEN_TILE
    nt = s // tm
    c = RWKV_DIM
    lc = RWKV_CHUNK
    gw = RWKV_PACK * RWKV_HEAD
    n_groups, n_chunks = c // gw, tm // lc
    halo_blocks = tm // 8
    last_halo = s // 8 - 1
    tok = lambda width: pl.BlockSpec((None, tm, width), lambda bi, t: (bi, t, 0))
    full = lambda a: pl.BlockSpec(a.shape, lambda bi, t: (0,) * a.ndim)
    in_specs = [
        tok(RWKV_IN),
        pl.BlockSpec((None, 8, RWKV_IN), lambda bi, t: (bi, jnp.maximum(t * halo_blocks - 1, 0), 0)),
        pl.BlockSpec((None, 8, RWKV_IN), lambda bi, t: (bi, jnp.minimum((t + 1) * halo_blocks, last_halo), 0)),
    ]
    args = [rw, rw, rw]
    if vres is not None:
        v_first, hd, vup, vb = vres
        in_specs += [tok(c), tok(LANES)]
        args += [v_first, hd]
    in_specs += [full(conv_w), full(wa), full(w0a0), full(k_k), full(k_a), full(r_k), full(g_up)]
    args += [conv_w, wa, w0a0, k_k, k_a, r_k, g_up]
    if vres is not None:
        in_specs += [full(vup), full(vb)]
        args += [vup, vb]
    in_specs += [full(gsum), full(tri), full(bdm)]
    args += [gsum, tri, bdm]
    op_shape = (2, b, nt, n_chunks, n_groups, 2 * lc, gw)
    op_spec = pl.BlockSpec((2, None, None, n_chunks, n_groups, 2 * lc, gw), lambda bi, t: (0, bi, t, 0, 0, 0, 0))
    out_specs = [op_spec, op_spec, tok(2 * c)]
    out_shape = [jax.ShapeDtypeStruct(op_shape, BF16), jax.ShapeDtypeStruct(op_shape, F32),
                 jax.ShapeDtypeStruct((b, s, 2 * c), F32)]
    if emit_v:
        out_specs.append(tok(c))
        out_shape.append(jax.ShapeDtypeStruct((b, s, c), F32))
    feat_buf = pltpu.VMEM((2, tm, c), F32)
    scratch = [feat_buf] * 6 + [pltpu.VMEM((tm, c), F32), pltpu.VMEM((2, 8 * (-(-n_chunks // 8)), c), F32)]
    outs = pl.pallas_call(
        functools.partial(_rwkv_ops_kernel, vres is not None, emit_v, n_ctx_tiles, nt),
        grid=(b, nt),
        in_specs=in_specs,
        out_specs=out_specs,
        out_shape=out_shape,
        scratch_shapes=scratch,
        compiler_params=_cparams(("parallel", "parallel")),
        name="rwkv7_chunk_ops",
    )(*args)
    rm, yc = outs[0], outs[1]

    tile = lambda d, i: _rwkv_tile(d, i, nt)
    carry_in = pl.BlockSpec((None, b, None, n_chunks, n_groups, 2 * lc, gw), lambda d, i: (d, 0, tile(d, i), 0, 0, 0, 0))
    y = pl.pallas_call(
        functools.partial(_rwkv_carry_kernel, nt),
        grid=(2, nt),
        in_specs=[carry_in, carry_in, pl.BlockSpec(bdm.shape, lambda d, i: (0, 0))],
        out_specs=pl.BlockSpec((None, b, tm, c), lambda d, i: (d, 0, tile(d, i), 0)),
        out_shape=jax.ShapeDtypeStruct((2, b, s, c), F32),
        scratch_shapes=[pltpu.VMEM((b, n_groups, lc, gw), F32)],
        compiler_params=_cparams(("parallel", "arbitrary")),
        name="rwkv7_state_carry",
    )(rm, yc, bdm)
    return (y,) + tuple(outs[2:])


def _merge_kernel(n_ctx_tiles, n_parts, *refs):
    x_refs, refs = refs[:n_parts], refs[n_parts:]
    (mod_ref, gain_ref, wgt_ref, oa_ref, y_ref, aux_ref, os_ref, lnw_ref, lnb_ref, gm_ref,
     woa_ref, wor_ref, wos_ref, wout_ref, o_ref) = refs
    d_model = o_ref.shape[-1]
    c = RWKV_DIM
    x = _stream_tile(n_ctx_tiles, x_refs)
    mod = mod_ref[...]
    hb = _rms_mod(x, gain_ref[...], mod[0:1], mod[1:2]).astype(BF16)
    gm = gm_ref[...]
    gates, o_rwkv = [], []

    def gate_stream():
        for j in range(N_BRANCH):
            gates.append(jax.nn.sigmoid(jnp.dot(hb, wgt_ref[:, j * d_model:(j + 1) * d_model],
                                                preferred_element_type=F32)))
            yield

    def rwkv_epilogue():
        y = y_ref[0] + y_ref[1]
        mu = _head_sums(y, gm)
        yield
        yc = y - mu
        var = _head_sums(yc * yc, gm)
        yield
        yn = yc * lax.rsqrt(var + GN_EPS) * lnw_ref[...] + lnb_ref[...]
        o_rwkv.append(((yn + aux_ref[:, 0:c]) * aux_ref[:, c:2 * c]).astype(BF16))
        yield

    _alternate(gate_stream(), rwkv_epilogue())
    m = (gates[0] * jnp.dot(oa_ref[...], woa_ref[...], preferred_element_type=F32)
         + gates[1] * jnp.dot(o_rwkv[0], wor_ref[...], preferred_element_type=F32)
         + gates[2] * jnp.dot(os_ref[...], wos_ref[...], preferred_element_type=F32))
    mix = _dot(m, wout_ref[...])
    o_ref[...] = x + mod[2:3] * mix


def _merge(parts, modt, gain, w_gt, o_attn, y, aux, o_sgu, ln_w, ln_b, gmean, woa, wor, wos, wout, n_ctx_tiles):
    b, d = parts[0].shape[0], parts[0].shape[2]
    s = sum(p.shape[1] for p in parts)
    tm = TOKEN_TILE
    tok = lambda width: pl.BlockSpec((None, tm, width), lambda bi, i: (bi, i, 0))
    tok2 = lambda width: pl.BlockSpec((2, None, tm, width), lambda bi, i: (0, bi, i, 0))
    full = lambda a: pl.BlockSpec(a.shape, lambda bi, i: (0,) * a.ndim)
    return pl.pallas_call(
        functools.partial(_merge_kernel, n_ctx_tiles, len(parts)),
        grid=(b, s // tm),
        in_specs=_stream_specs(parts, tm, n_ctx_tiles) + [
            pl.BlockSpec((None, None, 6, d), lambda bi, i: (bi, jnp.where(i < n_ctx_tiles, 0, 1), 0, 0)),
            full(gain), full(w_gt),
            tok(ATT_Q_DIM), tok2(RWKV_DIM), tok(2 * RWKV_DIM), tok(SGU_DIM),
            full(ln_w), full(ln_b), full(gmean), full(woa), full(wor), full(wos), full(wout),
        ],
        out_specs=tok(d),
        out_shape=jax.ShapeDtypeStruct((b, s, d), F32),
        compiler_params=_cparams(("parallel", "parallel")),
        name="merge_out",
    )(*parts, modt, gain, w_gt, o_attn, y, aux, o_sgu, ln_w, ln_b, gmean, woa, wor, wos, wout)


def _ffn_kernel(x_ref, mod_ref, gain_ref, w1_ref, w3_ref, w2_ref, o_ref):
    x = x_ref[...]
    mod = mod_ref[...]
    hb = _rms_mod(x, gain_ref[...], mod[3:4], mod[4:5]).astype(BF16)
    a = jnp.dot(hb, w1_ref[...], preferred_element_type=F32)
    bq = jnp.dot(hb, w3_ref[...], preferred_element_type=F32)
    hid = (a * jax.nn.sigmoid(a)) * bq
    o_ref[...] = x + mod[5:6] * _dot(hid, w2_ref[...])


def _ffn(xs, modt, gain, w1, w3, w2, n_ctx_tiles, first_tile):
    b, s, d = xs.shape
    tm = TOKEN_TILE
    nt = s // tm - first_tile
    full = lambda a: pl.BlockSpec(a.shape, lambda bi, i: (0,) * a.ndim)
    return pl.pallas_call(
        _ffn_kernel,
        grid=(b, nt),
        in_specs=[
            pl.BlockSpec((None, tm, d), lambda bi, i: (bi, i + first_tile, 0)),
            pl.BlockSpec((None, None, 6, d), lambda bi, i: (bi, jnp.where(i + first_tile < n_ctx_tiles, 0, 1), 0, 0)),
            full(gain), full(w1), full(w3), full(w2),
        ],
        out_specs=pl.BlockSpec((None, tm, d), lambda bi, i: (bi, i, 0)),
        out_shape=jax.ShapeDtypeStruct((b, nt * tm, d), F32),
        compiler_params=_cparams(("parallel", "parallel")),
        name="swiglu_ffn",
    )(xs, modt, gain, w1, w3, w2)


def _rope_tables(n_ctx, n_lat):
    rows = n_lat // GRID_W
    row = jnp.repeat(jnp.arange(rows), GRID_W).astype(F32)
    col = jnp.tile(jnp.arange(GRID_W), rows).astype(F32)
    inv = ROPE_BASE ** (-jnp.arange(ROPE_FREQS, dtype=F32) / ROPE_FREQS)
    ang = jnp.concatenate([row[:, None] * inv, col[:, None] * inv], -1)
    cos, sin = jnp.cos(ang), jnp.sin(ang)
    cos = jnp.concatenate([jnp.ones((n_ctx, HEAD_DIM // 2), F32), cos], 0)
    sin = jnp.concatenate([jnp.zeros((n_ctx, HEAD_DIM // 2), F32), sin], 0)
    return jnp.tile(jnp.concatenate([cos, cos], -1), (1, 2)), jnp.tile(jnp.concatenate([-sin, sin], -1), (1, 2))


def _block_diag_ones(n, blk, scale):
    i = np.arange(n)
    return jnp.asarray(((i[:, None] // blk) == (i[None, :] // blk)).astype(np.float32) * scale, dtype=BF16)


def _cumsum_matrices(tm, lc):
    i = np.arange(tm)
    same = (i[:, None] // lc) == (i[None, :] // lc)
    fwd = same & (i[None, :] <= i[:, None])
    bwd = same & (i[None, :] >= i[:, None])
    mats = np.stack([np.concatenate([fwd, same], 0), np.concatenate([bwd, same], 0)]).astype(np.float32)
    return jnp.asarray(mats, dtype=BF16)


def kernel(x, c, ctx, c_ctx, w_mod, b_mod, norm_mix, norm_ffn, w_in, q_gain, k_gain, attn_sink,
           rwkv_conv, rwkv_w0, rwkv_w_up, rwkv_a0, rwkv_a_up, rwkv_k_k, rwkv_k_a, rwkv_r_k, rwkv_g_up,
           rwkv_ln_w, rwkv_ln_b, rwkv_vres_down, rwkv_vres_up, rwkv_vres_b,
           sgu_ln_w, sgu_ln_b, sgu_w, sgu_b, w_o_attn, w_o_rwkv, w_o_sgu, w_out,
           ffn_w1, ffn_w3, ffn_w2):
    b, n_lat, d = x.shape
    n_ctx = ctx.shape[1]
    depth = w_in.shape[0]
    tm = TOKEN_TILE
    assert n_ctx % tm == 0 and n_lat % tm == 0 and n_lat % GRID_W == 0 and 2 * RWKV_CHUNK == LANES
    n_ctx_tiles = n_ctx // tm
    cdim = RWKV_DIM

    stream = (ctx, x)
    cos2, sin2 = _rope_tables(n_ctx, n_lat)
    gmean = _block_diag_ones(MXU_DIM, HEAD_DIM, 1.0 / HEAD_DIM)
    gsum = _block_diag_ones(MXU_DIM, RWKV_HEAD, 1.0)
    tri = _cumsum_matrices(tm, RWKV_CHUNK)
    bdm = _block_diag_ones(RWKV_PACK * RWKV_HEAD, RWKV_HEAD, 1.0)

    rows = -(-(b + 1) // 8) * 8
    cc = jnp.zeros((rows, d), F32).at[:b].set(c).at[b].set(c_ctx)
    mods = _modulation(cc, w_mod, b_mod)

    row2 = lambda a: a.reshape(1, -1)
    v_first = None
    for l in range(depth):
        last = l == depth - 1
        has_vres = l > 0
        mod_lat = mods[l, :b].reshape(b, 1, 6, d)
        mod_ctx = jnp.broadcast_to(mods[l, b].reshape(1, 1, 6, d), (b, 1, 6, d))
        modt = jnp.concatenate([mod_ctx, mod_lat], axis=1)

        n_proj = w_in.shape[2] - N_BRANCH * d
        w_ext = w_in[l, :, :n_proj]
        if has_vres:
            down = jnp.pad(rwkv_vres_down[l - 1], ((0, 0), (0, LANES - VRES_LORA)))
            w_ext = jnp.concatenate([w_ext, down], axis=1)
        w_ext = w_ext.astype(BF16)
        w_gt = w_in[l, :, n_proj:].astype(BF16)
        qg2 = jnp.tile(q_gain[l], 2).reshape(1, LANES)
        kg2 = jnp.tile(k_gain[l], 2).reshape(1, LANES)
        sgu_bb = jnp.broadcast_to(sgu_b[l][:, :, None], (SGU_GROUPS, SGU_CHUNK, SGU_DIM // SGU_GROUPS))
        outs = _proj_in(stream, modt, row2(norm_mix[l]), w_ext, cos2, sin2, qg2, kg2, gmean,
                        row2(sgu_ln_w[l]), row2(sgu_ln_b[l]), sgu_w[l].astype(BF16), sgu_bb, n_ctx_tiles, has_vres)
        q, kv, rw, o_sgu = outs[:4]

        o_attn = _attention(q, kv, attn_sink[l], n_ctx)

        zeros_lora = jnp.zeros((2, DECAY_LORA, cdim), F32)
        wa = jnp.concatenate([jnp.concatenate([rwkv_w_up[l], zeros_lora], axis=2),
                              jnp.concatenate([zeros_lora, rwkv_a_up[l]], axis=2)], axis=1).astype(BF16)
        w0a0 = jnp.concatenate([rwkv_w0[l], rwkv_a0[l]], axis=1).reshape(2, 1, 2 * cdim)
        vres = None
        if has_vres:
            vup = jnp.pad(rwkv_vres_up[l - 1], ((0, LANES - VRES_LORA), (0, 0))).astype(BF16)
            vres = (v_first, outs[4], vup, row2(rwkv_vres_b[l - 1]))
        r_outs = _rwkv(rw, rwkv_conv[l], wa, w0a0, row2(rwkv_k_k[l]), row2(rwkv_k_a[l]), row2(rwkv_r_k[l]),
                       rwkv_g_up[l].astype(BF16), gsum, tri, bdm, n_ctx_tiles, emit_v=(l == 0), vres=vres)
        y, aux = r_outs[0], r_outs[1]
        if l == 0:
            v_first = r_outs[2]

        xs = _merge(stream, modt, row2(norm_mix[l]), w_gt, o_attn, y, aux, o_sgu, row2(rwkv_ln_w[l]), row2(rwkv_ln_b[l]),
                    gmean, w_o_attn[l].astype(BF16), w_o_rwkv[l].astype(BF16), w_o_sgu[l].astype(BF16),
                    w_out[l].astype(BF16), n_ctx_tiles)
        xs = _ffn(xs, modt, row2(norm_ffn[l]), ffn_w1[l].astype(BF16), ffn_w3[l].astype(BF16),
                  ffn_w2[l].astype(BF16), n_ctx_tiles, first_tile=n_ctx_tiles if last else 0)
        stream = (xs,)
    return xs
```

```python
import functools
import math

import jax
import jax.numpy as jnp
import numpy as np
from jax import lax
from jax.experimental import pallas as pl
from jax.experimental.pallas import tpu as pltpu

HEAD_DIM = 64
ATT_HEADS = 8
ATT_KV_HEADS = 2
GQA_GROUP = ATT_HEADS // ATT_KV_HEADS
ATT_Q_DIM = ATT_HEADS * HEAD_DIM
ATT_KV_DIM = ATT_KV_HEADS * HEAD_DIM
ATT_BLOCK = 128
ATT_QBLOCKS = 2
ATT_SCALE = HEAD_DIM ** -0.5
ROPE_BASE = 10000.0
ROPE_FREQS = HEAD_DIM // 4
GRID_W = 64
RWKV_HEADS = 8
RWKV_HEAD = 64
RWKV_DIM = RWKV_HEADS * RWKV_HEAD
DECAY_LORA = 64
ICLR_LORA = 64
VRES_LORA = 32
GATE_LORA = 128
RWKV_IN = 3 * RWKV_DIM + DECAY_LORA + ICLR_LORA + GATE_LORA
GN_EPS = 64e-5
SGU_CHUNK = 128
SGU_GROUPS = 4
SGU_DIM = 512
N_BRANCH = 3
EPS = 1e-6

LANES = 128
SUBLANES = 8
TOKEN_TILE = 256
RWKV_CHUNK = 64
MXU_DIM = 256
RWKV_PACK = MXU_DIM // RWKV_HEAD
VMEM_LIMIT = 56 * 1024 * 1024

BF16 = jnp.bfloat16
F32 = jnp.float32


def _cparams(sem):
    return pltpu.CompilerParams(dimension_semantics=sem, vmem_limit_bytes=VMEM_LIMIT)


def _dot(a, b):
    return jnp.dot(a.astype(BF16), b.astype(BF16), preferred_element_type=F32)


def _split2(x):
    hi = x.astype(BF16)
    lo = (x - hi.astype(F32)).astype(BF16)
    return hi, lo


def _head_sums(x, g_bf16):
    w = g_bf16.shape[0]
    hi, lo = _split2(x)
    parts = [jnp.dot(hi[:, j:j + w], g_bf16, preferred_element_type=F32)
             + jnp.dot(lo[:, j:j + w], g_bf16, preferred_element_type=F32) for j in range(0, x.shape[1], w)]
    return parts[0] if len(parts) == 1 else jnp.concatenate(parts, axis=1)


def _alternate(*gens):
    live = list(gens)
    while live:
        for g in list(live):
            try:
                next(g)
            except StopIteration:
                live.remove(g)


def _rms_mod(x, gain, shift, scale):
    ms = jnp.mean(x * x, axis=-1, keepdims=True)
    return (x * lax.rsqrt(ms + EPS) * gain) * (1.0 + scale) + shift


def _mod_kernel(c_ref, w_ref, b_ref, o_ref):
    c = c_ref[...]
    o_ref[...] = _dot(c * jax.nn.sigmoid(c), w_ref[...]) + b_ref[...]


def _modulation(cc, w_mod, b_mod):
    nl, d, n6 = w_mod.shape
    rows = cc.shape[0]
    tn = n6 // 4
    assert n6 % (4 * LANES) == 0
    return pl.pallas_call(
        _mod_kernel,
        grid=(nl, n6 // tn),
        in_specs=[
            pl.BlockSpec((rows, d), lambda l, j: (0, 0)),
            pl.BlockSpec((None, d, tn), lambda l, j: (l, 0, j)),
            pl.BlockSpec((None, 1, tn), lambda l, j: (l, 0, j)),
        ],
        out_specs=pl.BlockSpec((None, rows, tn), lambda l, j: (l, 0, j)),
        out_shape=jax.ShapeDtypeStruct((nl, rows, n6), F32),
        compiler_params=_cparams(("parallel", "parallel")),
        name="modulation",
    )(cc, w_mod, b_mod.reshape(nl, 1, n6))


def _stream_tile(n_ctx_tiles, x_refs):
    if len(x_refs) == 1:
        return x_refs[0][...]
    return jnp.where(pl.program_id(1) < n_ctx_tiles, x_refs[0][...], x_refs[1][...])


def _stream_specs(parts, tm, n_ctx_tiles, first_tile=0):
    d = parts[0].shape[-1]
    if len(parts) == 1:
        return [pl.BlockSpec((None, tm, d), lambda bi, i: (bi, i + first_tile, 0))]
    assert first_tile == 0
    return [pl.BlockSpec((None, tm, d), lambda bi, i: (bi, jnp.minimum(i, n_ctx_tiles - 1), 0)),
            pl.BlockSpec((None, tm, d), lambda bi, i: (bi, jnp.maximum(i - n_ctx_tiles, 0), 0))]


def _proj_in_kernel(has_vres, n_ctx_tiles, n_parts, *refs):
    x_refs, refs = refs[:n_parts], refs[n_parts:]
    (mod_ref, gain_ref, w_ref, cos_ref, sin_ref, qg_ref, kg_ref, gm_ref, lnw_ref, lnb_ref, sw_ref, sb_ref,
     q_ref, kv_ref, rw_ref, os_ref, *maybe_hd_ref) = refs
    x = _stream_tile(n_ctx_tiles, x_refs)
    mod = mod_ref[...]
    hb = _rms_mod(x, gain_ref[...], mod[0:1], mod[1:2]).astype(BF16)
    cos = cos_ref[...]
    sin = sin_ref[...]
    gm = gm_ref[...]

    def norm_rope(p, gain):
        reps = p.shape[1] // LANES
        tile = lambda a: a if reps == 1 else jnp.concatenate([a] * reps, axis=1)
        lane = lax.broadcasted_iota(jnp.int32, p.shape, 1)
        first_half = (lane % HEAD_DIM) < (HEAD_DIM // 2)
        ms = _head_sums(p * p, gm[:p.shape[1], :p.shape[1]])
        t = p * lax.rsqrt(ms + EPS) * tile(gain)
        swapped = jnp.where(first_half, pltpu.roll(t, p.shape[1] - HEAD_DIM // 2, 1), pltpu.roll(t, HEAD_DIM // 2, 1))
        return t * tile(cos) + swapped * tile(sin)

    proj = lambda c0, width: jnp.dot(hb, w_ref[:, c0:c0 + width], preferred_element_type=F32)
    c_q, c_kv = 0, ATT_Q_DIM
    c_rw = c_kv + 2 * ATT_KV_DIM
    c_sg = c_rw + RWKV_IN
    c_hd = c_sg + 2 * SGU_DIM
    gd = SGU_DIM // SGU_GROUPS

    def rwkv_columns():
        for j in range(0, RWKV_IN, MXU_DIM):
            rw_ref[:, j:j + MXU_DIM] = proj(c_rw + j, MXU_DIM)
            yield
        if has_vres:
            maybe_hd_ref[0][...] = proj(c_hd, LANES)
        yield

    def epilogues():
        pq = [proj(c_q + j, MXU_DIM) for j in range(0, ATT_Q_DIM, MXU_DIM)]
        pkv = proj(c_kv, 2 * ATT_KV_DIM)
        yield
        pu = proj(c_sg, SGU_DIM)
        for j, p in enumerate(pq):
            q_ref[:, j * MXU_DIM:(j + 1) * MXU_DIM] = norm_rope(p, qg_ref[...]).astype(BF16)
            yield
        pvg = proj(c_sg + SGU_DIM, SGU_DIM)
        kv_ref[:, 0:ATT_KV_DIM] = norm_rope(pkv[:, 0:ATT_KV_DIM], kg_ref[...]).astype(BF16)
        kv_ref[:, ATT_KV_DIM:2 * ATT_KV_DIM] = pkv[:, ATT_KV_DIM:2 * ATT_KV_DIM].astype(BF16)
        yield
        u = jax.nn.gelu(pu)
        yield
        vg = jax.nn.gelu(pvg)
        yield
        mu = jnp.mean(vg, axis=-1, keepdims=True)
        var = jnp.mean(jnp.square(vg - mu), axis=-1, keepdims=True)
        vn = ((vg - mu) * lax.rsqrt(var + EPS) * lnw_ref[...] + lnb_ref[...]).astype(BF16)
        yield
        for ck in range(x.shape[0] // SGU_CHUNK):
            rows = slice(ck * SGU_CHUNK, (ck + 1) * SGU_CHUNK)
            for g in range(SGU_GROUPS):
                cols = slice(g * gd, (g + 1) * gd)
                sp = jnp.dot(sw_ref[g], vn[rows, cols], preferred_element_type=F32) + sb_ref[g]
                os_ref[rows, cols] = (u[rows, cols] * sp).astype(BF16)
            yield

    _alternate(epilogues(), rwkv_columns())


def _proj_in(parts, modt, gain, w_ext, cos2, sin2, qg, kg, gmean, ln_w, ln_b, sgu_w, sgu_bb, n_ctx_tiles, has_vres):
    b, d = parts[0].shape[0], parts[0].shape[2]
    s = sum(p.shape[1] for p in parts)
    tm = TOKEN_TILE
    nt = s // tm
    tok = lambda width: pl.BlockSpec((None, tm, width), lambda bi, i: (bi, i, 0))
    full = lambda a: pl.BlockSpec(a.shape, lambda bi, i: (0,) * a.ndim)
    out_shapes = [
        jax.ShapeDtypeStruct((b, s, ATT_Q_DIM), BF16),
        jax.ShapeDtypeStruct((b, s, 2 * ATT_KV_DIM), BF16),
        jax.ShapeDtypeStruct((b, s, RWKV_IN), F32),
        jax.ShapeDtypeStruct((b, s, SGU_DIM), BF16),
    ]
    out_specs = [tok(ATT_Q_DIM), tok(2 * ATT_KV_DIM), tok(RWKV_IN), tok(SGU_DIM)]
    if has_vres:
        out_shapes.append(jax.ShapeDtypeStruct((b, s, LANES), F32))
        out_specs.append(tok(LANES))
    return pl.pallas_call(
        functools.partial(_proj_in_kernel, has_vres, n_ctx_tiles, len(parts)),
        grid=(b, nt),
        in_specs=_stream_specs(parts, tm, n_ctx_tiles) + [
            pl.BlockSpec((None, None, 6, d), lambda bi, i: (bi, jnp.where(i < n_ctx_tiles, 0, 1), 0, 0)),
            full(gain), full(w_ext),
            pl.BlockSpec((tm, LANES), lambda bi, i: (i, 0)),
            pl.BlockSpec((tm, LANES), lambda bi, i: (i, 0)),
            full(qg), full(kg), full(gmean), full(ln_w), full(ln_b), full(sgu_w), full(sgu_bb),
        ],
        out_specs=out_specs,
        out_shape=out_shapes,
        compiler_params=_cparams(("parallel", "parallel")),
        name="proj_in",
    )(*parts, modt, gain, w_ext, cos2, sin2, qg, kg, gmean, ln_w, ln_b, sgu_w, sgu_bb)


def _attn_kernel(n_ctx_blocks, n_blocks, first_step, sink_ref, q_ref, kvc_ref, kvp_ref, kvo_ref, kvn_ref, o_ref):
    blk = ATT_BLOCK
    rows = GQA_GROUP * blk
    nband = 3 * blk
    qq = lax.broadcasted_iota(jnp.int32, (blk, nband), 0)
    kcol = lax.broadcasted_iota(jnp.int32, (blk, nband), 1)
    kk = kcol % blk
    rgrp = lax.broadcasted_iota(jnp.int32, (rows, 1), 0) // blk
    kvc = kvc_ref[...]
    kv_blocks = [kvp_ref[...]] + [kvo_ref[i * blk:(i + 1) * blk, :] for i in range(ATT_QBLOCKS)] + [kvn_ref[...]]
    nt_dims = (((1,), (1,)), ((), ()))
    outs = [[None] * ATT_HEADS for _ in range(ATT_QBLOCKS)]

    def chain(qb, kh):
        j = (pl.program_id(1) + first_step) * ATT_QBLOCKS + qb
        is_lat = j >= n_ctx_blocks
        ok_prev = j >= n_ctx_blocks + 1
        ok_next = jnp.logical_and(is_lat, j <= n_blocks - 2)
        off_prev = jnp.where(ok_prev, 0, blk)
        off_next = jnp.where(ok_next, 0, blk)
        own_end = jnp.where(is_lat, 2 * blk, 0)
        band_mask = (((kcol < blk) & (kk >= qq + off_prev))
                     | ((kcol >= blk) & (kcol < own_end))
                     | ((kcol >= 2 * blk) & (kk <= qq - off_next)))
        band_bias = jnp.where(band_mask, 0.0, -jnp.inf)
        band_bias = jnp.concatenate([band_bias] * GQA_GROUP, axis=0)
        h0 = kh * GQA_GROUP
        q = q_ref[qb * blk:(qb + 1) * blk, :] * ATT_SCALE
        qh = jnp.concatenate([q[:, (h0 + g) * HEAD_DIM:(h0 + g + 1) * HEAD_DIM] for g in range(GQA_GROUP)], axis=0)
        ksl = slice(kh * HEAD_DIM, (kh + 1) * HEAD_DIM)
        vsl = slice(ATT_KV_DIM + kh * HEAD_DIM, ATT_KV_DIM + (kh + 1) * HEAD_DIM)
        band = kv_blocks[qb:qb + 3]
        k_band = jnp.concatenate([blkv[:, ksl] for blkv in band], axis=0)
        v_band = jnp.concatenate([blkv[:, vsl] for blkv in band], axis=0)
        s_ctx = lax.dot_general(qh, kvc[:, ksl], nt_dims, preferred_element_type=F32)
        s_band = lax.dot_general(qh, k_band, nt_dims, preferred_element_type=F32) + band_bias
        yield
        sink = jnp.zeros((rows, 1), F32)
        for g in range(GQA_GROUP):
            sink = jnp.where(rgrp == g, sink_ref[h0 + g], sink)
        m = jnp.maximum(jnp.maximum(jnp.max(s_ctx, axis=-1, keepdims=True), jnp.max(s_band, axis=-1, keepdims=True)), sink)
        yield
        p_ctx = jnp.exp(s_ctx - m)
        yield
        p_band = jnp.exp(s_band - m)
        yield
        denom = (jnp.sum(p_ctx, axis=-1, keepdims=True) + jnp.sum(p_band, axis=-1, keepdims=True)) + jnp.exp(sink - m)
        yield
        o = (jnp.dot(p_ctx.astype(BF16), kvc[:, vsl], preferred_element_type=F32)
             + jnp.dot(p_band.astype(BF16), v_band, preferred_element_type=F32)) / denom
        for g in range(GQA_GROUP):
            outs[qb][h0 + g] = o[g * blk:(g + 1) * blk]
        yield

    _alternate(*[chain(qb, kh) for qb in range(ATT_QBLOCKS) for kh in range(ATT_KV_HEADS)])
    for qb in range(ATT_QBLOCKS):
        o_ref[qb * blk:(qb + 1) * blk, :] = jnp.concatenate(outs[qb], axis=1).astype(BF16)


def _attention(q, kv, sink, n_ctx, first_block=0):
    b, s, _ = q.shape
    blk = ATT_BLOCK
    nb = s // blk
    ncb = n_ctx // blk
    kvw = kv.shape[-1]
    nq = ATT_QBLOCKS
    assert nb % nq == 0 and ncb % nq == 0 and first_block % nq == 0
    fs = first_block // nq
    return pl.pallas_call(
        functools.partial(_attn_kernel, ncb, nb, fs),
        grid=(b, nb // nq - fs),
        in_specs=[
            pl.BlockSpec(memory_space=pltpu.SMEM),
            pl.BlockSpec((None, nq * blk, ATT_Q_DIM), lambda bi, j: (bi, j + fs, 0)),
            pl.BlockSpec((None, n_ctx, kvw), lambda bi, j: (bi, 0, 0)),
            pl.BlockSpec((None, blk, kvw), lambda bi, j: (bi, jnp.maximum(nq * (j + fs) - 1, 0), 0)),
            pl.BlockSpec((None, nq * blk, kvw), lambda bi, j: (bi, j + fs, 0)),
            pl.BlockSpec((None, blk, kvw), lambda bi, j: (bi, jnp.minimum(nq * (j + fs) + nq, nb - 1), 0)),
        ],
        out_specs=pl.BlockSpec((None, nq * blk, ATT_Q_DIM), lambda bi, j: (bi, j, 0)),
        out_shape=jax.ShapeDtypeStruct((b, s - first_block * blk, ATT_Q_DIM), BF16),
        compiler_params=_cparams(("parallel", "parallel")),
        name="window_attention",
    )(sink, q, kv, kv, kv, kv)


def _rwkv_tile(d, i, n_tiles):
    return jnp.where(d == 0, i, jnp.where(i == 0, 0, n_tiles - i))


def _rwkv_ops_kernel(has_vres, emit_v, n_ctx_tiles, n_tiles, *refs):
    it = iter(refs)
    rw_ref, rwp_ref, rwn_ref = next(it), next(it), next(it)
    vf_ref = hd_ref = vup_ref = vb_ref = None
    if has_vres:
        vf_ref, hd_ref = next(it), next(it)
    conv_ref, wa_ref, w0a0_ref, kkw_ref, kaw_ref, rkw_ref, gup_ref = (next(it) for _ in range(7))
    if has_vres:
        vup_ref, vb_ref = next(it), next(it)
    gsum_ref, tri_ref, bdm_ref = next(it), next(it), next(it)
    rm_ref, yc_ref, aux_ref = next(it), next(it), next(it)
    v_out_ref = next(it) if emit_v else None
    fa_ref, fr_ref, fb_ref, fk_ref, fbh_ref, fkh_ref, fv_ref, fel_ref = (next(it) for _ in range(8))

    t = pl.program_id(1)
    tm = rw_ref.shape[0]
    c_dim = RWKV_DIM
    lc = RWKV_CHUNK
    n_chunks = tm // lc
    gw = RWKV_PACK * RWKV_HEAD
    n_groups = c_dim // gw

    x = rw_ref[...]
    ok_prev = jnp.logical_and(t != 0, t != n_ctx_tiles)
    ok_next = jnp.logical_and(t != n_ctx_tiles - 1, t != n_tiles - 1)
    halo_p = jnp.where(ok_prev, rwp_ref[SUBLANES - 1:SUBLANES, :], 0.0)
    halo_n = jnp.where(ok_next, rwn_ref[0:1, :], 0.0)
    edge = lax.broadcasted_iota(jnp.int32, (SUBLANES, x.shape[1]), 0)
    xp = pltpu.roll(x, 1, 0)
    xp = jnp.concatenate([jnp.where(edge == 0, halo_p, xp[0:SUBLANES]), xp[SUBLANES:]], axis=0)
    xn = pltpu.roll(x, tm - 1, 0)
    xn = jnp.concatenate([xn[0:tm - SUBLANES], jnp.where(edge == SUBLANES - 1, halo_n, xn[tm - SUBLANES:])], axis=0)
    cw = conv_ref[...]
    cv = (xp * cw[0:1] + x * cw[1:2]) + xn * cw[2:3]
    r = cv[:, 0:c_dim]
    k = cv[:, c_dim:2 * c_dim]
    v = cv[:, 2 * c_dim:3 * c_dim]
    xwa = cv[:, 3 * c_dim:3 * c_dim + LANES]
    xg = cv[:, 3 * c_dim + LANES:3 * c_dim + 2 * LANES]

    if emit_v:
        v_out_ref[...] = v
    if has_vres:
        gate = jax.nn.sigmoid(vb_ref[...] + _dot(hd_ref[...], vup_ref[...]))
        v = v + (vf_ref[...] - v) * gate
    fv_ref[...] = v

    gsum = gsum_ref[...]
    kk = k * kkw_ref[...]
    kk = kk * lax.rsqrt(_head_sums(kk * kk, gsum) + 1e-12)
    lane = lax.broadcasted_iota(jnp.int32, xwa.shape, 1)
    lora_in = jnp.where(lane < DECAY_LORA, jnp.tanh(xwa), xwa).astype(BF16)
    kd_sum = None
    for d in range(2):
        pre = jnp.dot(lora_in, wa_ref[d], preferred_element_type=F32) + w0a0_ref[d]
        a_sig = jax.nn.sigmoid(pre[:, c_dim:2 * c_dim])
        lw = -jnp.exp(-jax.nn.softplus(-pre[:, 0:c_dim]) - 0.5)
        kd = k * (1.0 + (a_sig - 1.0) * kaw_ref[...])
        kd_sum = kd if d == 0 else kd_sum + kd
        bvec = kk * a_sig
        hi, lo = _split2(lw)
        cs = jnp.dot(tri_ref[d], hi, preferred_element_type=F32) + jnp.dot(tri_ref[d], lo, preferred_element_type=F32)
        c_inc = cs[0:tm]
        c_tot = cs[tm:2 * tm]
        e_inc = jnp.exp(c_inc)
        e_neg = jnp.exp(-c_inc)
        e_rem = jnp.exp(c_tot - c_inc)
        fa_ref[d] = -kk * jnp.exp(c_inc - lw)
        fr_ref[d] = r * e_inc
        fb_ref[d] = bvec * e_neg
        fk_ref[d] = kd * e_neg
        fbh_ref[d] = bvec * e_rem
        fkh_ref[d] = kd * e_rem
        for c in range(n_chunks):
            fel_ref[d, c:c + 1, :] = jnp.exp(c_tot[c * lc:c * lc + 1, :])
    aux_ref[:, 0:c_dim] = _head_sums(r * kd_sum * rkw_ref[...], gsum) * v
    aux_ref[:, c_dim:2 * c_dim] = _dot(jax.nn.sigmoid(xg), gup_ref[...])

    bdm = bdm_ref[...]

    def bd(xb):
        return jnp.concatenate([xb] * RWKV_PACK, axis=0) * bdm

    low_half = lax.broadcasted_iota(jnp.int32, (lc, 2 * lc), 1) < lc

    def head_transpose(xf):
        w = jnp.concatenate([xf, xf], axis=0).T
        pairs = [jnp.where(low_half, w[(2 * j) * lc:(2 * j + 1) * lc], w[(2 * j + 1) * lc:(2 * j + 2) * lc])
                 for j in range(RWKV_PACK // 2)]
        return jnp.concatenate(pairs, axis=1)

    mm = lambda p, q: jnp.dot(p, q, preferred_element_type=F32)
    ri = lax.broadcasted_iota(jnp.int32, (2 * lc, 2 * gw), 0)
    ci = lax.broadcasted_iota(jnp.int32, (2 * lc, 2 * gw), 1) % lc
    amask = [jnp.logical_or(jnp.logical_and(ri < lc, sgn * (ri - ci) > 0),
                            jnp.logical_and(ri >= lc, sgn * (ri - lc - ci) >= 0)) for sgn in (1, -1)]
    eye_p = (lax.broadcasted_iota(jnp.int32, (lc, gw), 0)
             == lax.broadcasted_iota(jnp.int32, (lc, gw), 1) % lc).astype(F32)
    n_levels = int(math.log2(lc))

    probs = [(d, c, g) for d in range(2) for c in range(n_chunks) for g in range(n_groups)]
    blk = lambda ref, p: ref[p[0], p[1] * lc:(p[1] + 1) * lc, p[2] * gw:(p[2] + 1) * gw]
    vblk = lambda p: fv_ref[p[1] * lc:(p[1] + 1) * lc, p[2] * gw:(p[2] + 1) * gw]
    ab = [blk(fa_ref, p).astype(BF16) for p in probs]
    vb = [vblk(p).astype(BF16) for p in probs]
    a_all = []
    for j, p in enumerate(probs):
        lhs = jnp.concatenate([ab[j], blk(fr_ref, p).astype(BF16)], axis=0)
        rhs_t = jnp.concatenate([bd(blk(fb_ref, p).astype(BF16)), bd(blk(fk_ref, p).astype(BF16))], axis=0)
        full = lax.dot_general(lhs, rhs_t, (((1,), (1,)), ((), ())), preferred_element_type=F32)
        a_all.append(jnp.where(amask[p[0]], full, 0.0))
    a_rb = [m[lc:2 * lc, 0:gw].astype(BF16) for m in a_all]
    a_rk = [m[lc:2 * lc, gw:2 * gw].astype(BF16) for m in a_all]
    t_p = [eye_p + m[0:lc, 0:gw] for m in a_all]
    pb = [m[0:lc, 0:gw].astype(BF16) for m in a_all]
    pb = [mm(p, bd(p)).astype(BF16) for p in pb]
    for lvl in range(1, n_levels):
        if lvl < n_levels - 1:
            res = [mm(jnp.concatenate([p, t.astype(BF16)], axis=0), bd(p)) for p, t in zip(pb, t_p)]
            pb = [m[0:lc].astype(BF16) for m in res]
            t_p = [t + m[lc:2 * lc] for t, m in zip(t_p, res)]
        else:
            t_p = [t + mm(t.astype(BF16), bd(p)) for p, t in zip(pb, t_p)]
    u = [mm(m[0:lc, gw:2 * gw].astype(BF16), bd(vv)) for m, vv in zip(a_all, vb)]
    xs = [mm(t.astype(BF16), jnp.concatenate([bd(a), bd(uu.astype(BF16))], axis=1))
          for t, a, uu in zip(t_p, ab, u)]
    a1b = [m[:, 0:gw].astype(BF16) for m in xs]
    u1b = [m[:, gw:2 * gw].astype(BF16) for m in xs]
    for j, p in enumerate(probs):
        d, c, g = p
        zb = head_transpose(blk(fbh_ref, p)).astype(BF16)
        zk = head_transpose(blk(fkh_ref, p)).astype(BF16)
        res = mm(jnp.concatenate([a_rb[j], zb], axis=0), bd(a1b[j]))
        el_c = fel_ref[d, c:c + 1, g * gw:(g + 1) * gw]
        rm_ref[d, c, g] = jnp.concatenate([blk(fr_ref, p) + res[0:lc], res[lc:2 * lc] + eye_p * el_c],
                                          axis=0).astype(BF16)
        yc_ref[d, c, g] = mm(jnp.concatenate([jnp.concatenate([a_rb[j], a_rk[j]], axis=1),
                                              jnp.concatenate([zb, zk], axis=1)], axis=0),
                             jnp.concatenate([bd(u1b[j]), bd(vb[j])], axis=0))


def _rwkv_carry_kernel(rm_ref, yc_ref, bdm_ref, y_ref, st_ref):
    d = pl.program_id(0)
    i = pl.program_id(1)
    nb, n_chunks, n_groups, lc2, gw = rm_ref.shape
    lc = lc2 // 2
    bdm = bdm_ref[...]

    @pl.when(i == 0)
    def _():
        st_ref[...] = jnp.zeros_like(st_ref)

    for n in range(n_chunks):
        cc = jnp.where(d == 0, n, n_chunks - 1 - n)
        rows = pl.ds(pl.multiple_of(cc * lc, lc), lc)
        for bi in range(nb):
            for g in range(n_groups):
                st_bd = jnp.concatenate([st_ref[bi, g].astype(BF16)] * RWKV_PACK, axis=0) * bdm
                tot = jnp.dot(rm_ref[bi, cc, g], st_bd, preferred_element_type=F32) + yc_ref[bi, cc, g]
                y_ref[bi, rows, g * gw:(g + 1) * gw] = tot[0:lc]
                st_ref[bi, g] = tot[lc:2 * lc]


def _rwkv(rw, conv_w, wa, w0a0, k_k, k_a, r_k, g_up, gsum, tri, bdm, n_ctx_tiles, emit_v, vres=None):
    b, s, _ = rw.shape
    tm = TOKEN_TILE
    nt = s // tm
    c = RWKV_DIM
    lc = RWKV_CHUNK
    gw = RWKV_PACK * RWKV_HEAD
    n_groups, n_chunks = c // gw, tm // lc
    halo_blocks = tm // SUBLANES
    last_halo = s // SUBLANES - 1
    tok = lambda width: pl.BlockSpec((None, tm, width), lambda bi, t: (bi, t, 0))
    full = lambda a: pl.BlockSpec(a.shape, lambda bi, t: (0,) * a.ndim)
    in_specs = [
        tok(RWKV_IN),
        pl.BlockSpec((None, SUBLANES, RWKV_IN), lambda bi, t: (bi, jnp.maximum(t * halo_blocks - 1, 0), 0)),
        pl.BlockSpec((None, SUBLANES, RWKV_IN), lambda bi, t: (bi, jnp.minimum((t + 1) * halo_blocks, last_halo), 0)),
    ]
    args = [rw, rw, rw]
    if vres is not None:
        v_first, hd, vup, vb = vres
        in_specs += [tok(c), tok(LANES)]
        args += [v_first, hd]
    in_specs += [full(conv_w), full(wa), full(w0a0), full(k_k), full(k_a), full(r_k), full(g_up)]
    args += [conv_w, wa, w0a0, k_k, k_a, r_k, g_up]
    if vres is not None:
        in_specs += [full(vup), full(vb)]
        args += [vup, vb]
    in_specs += [full(gsum), full(tri), full(bdm)]
    args += [gsum, tri, bdm]
    op_shape = (2, b, nt, n_chunks, n_groups, 2 * lc, gw)
    op_spec = pl.BlockSpec((2, None, None, n_chunks, n_groups, 2 * lc, gw), lambda bi, t: (0, bi, t, 0, 0, 0, 0))
    out_specs = [op_spec, op_spec, tok(2 * c)]
    out_shape = [jax.ShapeDtypeStruct(op_shape, BF16), jax.ShapeDtypeStruct(op_shape, F32),
                 jax.ShapeDtypeStruct((b, s, 2 * c), F32)]
    if emit_v:
        out_specs.append(tok(c))
        out_shape.append(jax.ShapeDtypeStruct((b, s, c), F32))
    feat_buf = pltpu.VMEM((2, tm, c), F32)
    scratch = [feat_buf] * 6 + [pltpu.VMEM((tm, c), F32), pltpu.VMEM((2, SUBLANES * (-(-n_chunks // SUBLANES)), c), F32)]
    outs = pl.pallas_call(
        functools.partial(_rwkv_ops_kernel, vres is not None, emit_v, n_ctx_tiles, nt),
        grid=(b, nt),
        in_specs=in_specs,
        out_specs=out_specs,
        out_shape=out_shape,
        scratch_shapes=scratch,
        compiler_params=_cparams(("parallel", "parallel")),
        name="rwkv7_chunk_ops",
    )(*args)
    rm, yc = outs[0], outs[1]

    tile = lambda d, i: _rwkv_tile(d, i, nt)
    carry_in = pl.BlockSpec((None, b, None, n_chunks, n_groups, 2 * lc, gw), lambda d, i: (d, 0, tile(d, i), 0, 0, 0, 0))
    y = pl.pallas_call(
        _rwkv_carry_kernel,
        grid=(2, nt),
        in_specs=[carry_in, carry_in, pl.BlockSpec(bdm.shape, lambda d, i: (0, 0))],
        out_specs=pl.BlockSpec((None, b, tm, c), lambda d, i: (d, 0, tile(d, i), 0)),
        out_shape=jax.ShapeDtypeStruct((2, b, s, c), F32),
        scratch_shapes=[pltpu.VMEM((b, n_groups, lc, gw), F32)],
        compiler_params=_cparams(("parallel", "arbitrary")),
        name="rwkv7_state_carry",
    )(rm, yc, bdm)
    return (y,) + tuple(outs[2:])


def _merge_kernel(n_ctx_tiles, n_parts, *refs):
    x_refs, refs = refs[:n_parts], refs[n_parts:]
    (mod_ref, gain_ref, wgt_ref, oa_ref, y_ref, aux_ref, os_ref, lnw_ref, lnb_ref, gm_ref,
     woa_ref, wor_ref, wos_ref, wout_ref, o_ref) = refs
    d_model = o_ref.shape[-1]
    c = RWKV_DIM
    x = _stream_tile(n_ctx_tiles, x_refs)
    mod = mod_ref[...]
    hb = _rms_mod(x, gain_ref[...], mod[0:1], mod[1:2]).astype(BF16)
    gm = gm_ref[...]
    gates, o_rwkv = [], []

    def gate_stream():
        for j in range(N_BRANCH):
            gates.append(jax.nn.sigmoid(jnp.dot(hb, wgt_ref[:, j * d_model:(j + 1) * d_model],
                                                preferred_element_type=F32)))
            yield

    def rwkv_epilogue():
        y = y_ref[0] + y_ref[1]
        mu = _head_sums(y, gm)
        yield
        yc = y - mu
        var = _head_sums(yc * yc, gm)
        yield
        yn = yc * lax.rsqrt(var + GN_EPS) * lnw_ref[...] + lnb_ref[...]
        o_rwkv.append(((yn + aux_ref[:, 0:c]) * aux_ref[:, c:2 * c]).astype(BF16))
        yield

    _alternate(gate_stream(), rwkv_epilogue())
    m = (gates[0] * jnp.dot(oa_ref[...], woa_ref[...], preferred_element_type=F32)
         + gates[1] * jnp.dot(o_rwkv[0], wor_ref[...], preferred_element_type=F32)
         + gates[2] * jnp.dot(os_ref[...], wos_ref[...], preferred_element_type=F32))
    mix = _dot(m, wout_ref[...])
    o_ref[...] = x + mod[2:3] * mix


def _merge(parts, modt, gain, w_gt, o_attn, y, aux, o_sgu, ln_w, ln_b, gmean, woa, wor, wos, wout, n_ctx_tiles,
           first_tile=0):
    b, d = parts[0].shape[0], parts[0].shape[2]
    s = sum(p.shape[1] for p in parts)
    tm = TOKEN_TILE
    nt = s // tm - first_tile
    assert o_attn.shape[1] == nt * tm
    tok = lambda width: pl.BlockSpec((None, tm, width), lambda bi, i: (bi, i + first_tile, 0))
    tok2 = lambda width: pl.BlockSpec((2, None, tm, width), lambda bi, i: (0, bi, i + first_tile, 0))
    own = lambda width: pl.BlockSpec((None, tm, width), lambda bi, i: (bi, i, 0))
    full = lambda a: pl.BlockSpec(a.shape, lambda bi, i: (0,) * a.ndim)
    return pl.pallas_call(
        functools.partial(_merge_kernel, n_ctx_tiles, len(parts)),
        grid=(b, nt),
        in_specs=_stream_specs(parts, tm, n_ctx_tiles, first_tile) + [
            pl.BlockSpec((None, None, 6, d), lambda bi, i: (bi, jnp.where(i + first_tile < n_ctx_tiles, 0, 1), 0, 0)),
            full(gain), full(w_gt),
            own(ATT_Q_DIM), tok2(RWKV_DIM), tok(2 * RWKV_DIM), tok(SGU_DIM),
            full(ln_w), full(ln_b), full(gmean), full(woa), full(wor), full(wos), full(wout),
        ],
        out_specs=own(d),
        out_shape=jax.ShapeDtypeStruct((b, nt * tm, d), F32),
        compiler_params=_cparams(("parallel", "parallel")),
        name="merge_out",
    )(*parts, modt, gain, w_gt, o_attn, y, aux, o_sgu, ln_w, ln_b, gmean, woa, wor, wos, wout)


def _ffn_kernel(x_ref, mod_ref, gain_ref, w1_ref, w3_ref, w2_ref, o_ref):
    x = x_ref[...]
    mod = mod_ref[...]
    hb = _rms_mod(x, gain_ref[...], mod[3:4], mod[4:5]).astype(BF16)
    a = jnp.dot(hb, w1_ref[...], preferred_element_type=F32)
    bq = jnp.dot(hb, w3_ref[...], preferred_element_type=F32)
    hid = (a * jax.nn.sigmoid(a)) * bq
    o_ref[...] = x + mod[5:6] * _dot(hid, w2_ref[...])


def _ffn(xs, modt, gain, w1, w3, w2, n_ctx_tiles, first_tile):
    b, s, d = xs.shape
    tm = TOKEN_TILE
    nt = s // tm - first_tile
    full = lambda a: pl.BlockSpec(a.shape, lambda bi, i: (0,) * a.ndim)
    return pl.pallas_call(
        _ffn_kernel,
        grid=(b, nt),
        in_specs=[
            pl.BlockSpec((None, tm, d), lambda bi, i: (bi, i + first_tile, 0)),
            pl.BlockSpec((None, None, 6, d), lambda bi, i: (bi, jnp.where(i + first_tile < n_ctx_tiles, 0, 1), 0, 0)),
            full(gain), full(w1), full(w3), full(w2),
        ],
        out_specs=pl.BlockSpec((None, tm, d), lambda bi, i: (bi, i, 0)),
        out_shape=jax.ShapeDtypeStruct((b, nt * tm, d), F32),
        compiler_params=_cparams(("parallel", "parallel")),
        name="swiglu_ffn",
    )(xs, modt, gain, w1, w3, w2)


def _rope_tables(n_ctx, n_lat):
    rows = n_lat // GRID_W
    row = jnp.repeat(jnp.arange(rows), GRID_W).astype(F32)
    col = jnp.tile(jnp.arange(GRID_W), rows).astype(F32)
    inv = ROPE_BASE ** (-jnp.arange(ROPE_FREQS, dtype=F32) / ROPE_FREQS)
    ang = jnp.concatenate([row[:, None] * inv, col[:, None] * inv], -1)
    cos, sin = jnp.cos(ang), jnp.sin(ang)
    cos = jnp.concatenate([jnp.ones((n_ctx, HEAD_DIM // 2), F32), cos], 0)
    sin = jnp.concatenate([jnp.zeros((n_ctx, HEAD_DIM // 2), F32), sin], 0)
    return jnp.tile(jnp.concatenate([cos, cos], -1), (1, 2)), jnp.tile(jnp.concatenate([-sin, sin], -1), (1, 2))


def _block_diag_ones(n, blk, scale):
    i = np.arange(n)
    return jnp.asarray(((i[:, None] // blk) == (i[None, :] // blk)).astype(np.float32) * scale, dtype=BF16)


def _cumsum_matrices(tm, lc):
    i = np.arange(tm)
    same = (i[:, None] // lc) == (i[None, :] // lc)
    fwd = same & (i[None, :] <= i[:, None])
    bwd = same & (i[None, :] >= i[:, None])
    mats = np.stack([np.concatenate([fwd, same], 0), np.concatenate([bwd, same], 0)]).astype(np.float32)
    return jnp.asarray(mats, dtype=BF16)


def kernel(x, c, ctx, c_ctx, w_mod, b_mod, norm_mix, norm_ffn, w_in, q_gain, k_gain, attn_sink,
           rwkv_conv, rwkv_w0, rwkv_w_up, rwkv_a0, rwkv_a_up, rwkv_k_k, rwkv_k_a, rwkv_r_k, rwkv_g_up,
           rwkv_ln_w, rwkv_ln_b, rwkv_vres_down, rwkv_vres_up, rwkv_vres_b,
           sgu_ln_w, sgu_ln_b, sgu_w, sgu_b, w_o_attn, w_o_rwkv, w_o_sgu, w_out,
           ffn_w1, ffn_w3, ffn_w2):
    b, n_lat, d = x.shape
    n_ctx = ctx.shape[1]
    depth = w_in.shape[0]
    tm = TOKEN_TILE
    assert n_ctx % tm == 0 and n_lat % tm == 0 and n_lat % GRID_W == 0 and 2 * RWKV_CHUNK == LANES
    n_ctx_tiles = n_ctx // tm
    cdim = RWKV_DIM

    stream = (ctx, x)
    cos2, sin2 = _rope_tables(n_ctx, n_lat)
    gmean = _block_diag_ones(MXU_DIM, HEAD_DIM, 1.0 / HEAD_DIM)
    gsum = _block_diag_ones(MXU_DIM, RWKV_HEAD, 1.0)
    tri = _cumsum_matrices(tm, RWKV_CHUNK)
    bdm = _block_diag_ones(RWKV_PACK * RWKV_HEAD, RWKV_HEAD, 1.0)

    rows = -(-(b + 1) // SUBLANES) * SUBLANES
    cc = jnp.zeros((rows, d), F32).at[:b].set(c).at[b].set(c_ctx)
    mods = _modulation(cc, w_mod, b_mod)

    row2 = lambda a: a.reshape(1, -1)
    v_first = None
    for l in range(depth):
        last = l == depth - 1
        has_vres = l > 0
        mod_lat = mods[l, :b].reshape(b, 1, 6, d)
        mod_ctx = jnp.broadcast_to(mods[l, b].reshape(1, 1, 6, d), (b, 1, 6, d))
        modt = jnp.concatenate([mod_ctx, mod_lat], axis=1)

        n_proj = w_in.shape[2] - N_BRANCH * d
        w_ext = w_in[l, :, :n_proj]
        if has_vres:
            down = jnp.pad(rwkv_vres_down[l - 1], ((0, 0), (0, LANES - VRES_LORA)))
            w_ext = jnp.concatenate([w_ext, down], axis=1)
        w_ext = w_ext.astype(BF16)
        w_gt = w_in[l, :, n_proj:].astype(BF16)
        qg2 = jnp.tile(q_gain[l], 2).reshape(1, LANES)
        kg2 = jnp.tile(k_gain[l], 2).reshape(1, LANES)
        sgu_bb = jnp.broadcast_to(sgu_b[l][:, :, None], (SGU_GROUPS, SGU_CHUNK, SGU_DIM // SGU_GROUPS))
        outs = _proj_in(stream, modt, row2(norm_mix[l]), w_ext, cos2, sin2, qg2, kg2, gmean,
                        row2(sgu_ln_w[l]), row2(sgu_ln_b[l]), sgu_w[l].astype(BF16), sgu_bb, n_ctx_tiles, has_vres)
        q, kv, rw, o_sgu = outs[:4]

        skip = n_ctx_tiles if last and len(stream) == 1 else 0
        o_attn = _attention(q, kv, attn_sink[l], n_ctx, first_block=skip * (tm // ATT_BLOCK))

        zeros_lora = jnp.zeros((2, DECAY_LORA, cdim), F32)
        wa = jnp.concatenate([jnp.concatenate([rwkv_w_up[l], zeros_lora], axis=2),
                              jnp.concatenate([zeros_lora, rwkv_a_up[l]], axis=2)], axis=1).astype(BF16)
        w0a0 = jnp.concatenate([rwkv_w0[l], rwkv_a0[l]], axis=1).reshape(2, 1, 2 * cdim)
        vres = None
        if has_vres:
            vup = jnp.pad(rwkv_vres_up[l - 1], ((0, LANES - VRES_LORA), (0, 0))).astype(BF16)
            vres = (v_first, outs[4], vup, row2(rwkv_vres_b[l - 1]))
        r_outs = _rwkv(rw, rwkv_conv[l], wa, w0a0, row2(rwkv_k_k[l]), row2(rwkv_k_a[l]), row2(rwkv_r_k[l]),
                       rwkv_g_up[l].astype(BF16), gsum, tri, bdm, n_ctx_tiles, emit_v=(l == 0), vres=vres)
        y, aux = r_outs[0], r_outs[1]
        if l == 0:
            v_first = r_outs[2]

        xs = _merge(stream, modt, row2(norm_mix[l]), w_gt, o_attn, y, aux, o_sgu, row2(rwkv_ln_w[l]), row2(rwkv_ln_b[l]),
                    gmean, w_o_attn[l].astype(BF16), w_o_rwkv[l].astype(BF16), w_o_sgu[l].astype(BF16),
                    w_out[l].astype(BF16), n_ctx_tiles, first_tile=skip)
        xs = _ffn(xs, modt, row2(norm_ffn[l]), ffn_w1[l].astype(BF16), ffn_w3[l].astype(BF16), ffn_w2[l].astype(BF16),
                  n_ctx_tiles - skip, first_tile=n_ctx_tiles - skip if last else 0)
        stream = (xs,)
    return xs
```

```python
import functools
import math

import jax
import jax.numpy as jnp
import numpy as np
from jax import lax
from jax.experimental import pallas as pl
from jax.experimental.pallas import tpu as pltpu

HEAD_DIM = 64
ATT_HEADS = 8
ATT_KV_HEADS = 2
GQA_GROUP = ATT_HEADS // ATT_KV_HEADS
ATT_Q_DIM = ATT_HEADS * HEAD_DIM
ATT_KV_DIM = ATT_KV_HEADS * HEAD_DIM
ATT_BLOCK = 128
ATT_QBLOCKS = 2
ATT_SCALE = HEAD_DIM ** -0.5
ROPE_BASE = 10000.0
ROPE_FREQS = HEAD_DIM // 4
GRID_W = 64
RWKV_HEADS = 8
RWKV_HEAD = 64
RWKV_DIM = RWKV_HEADS * RWKV_HEAD
DECAY_LORA = 64
ICLR_LORA = 64
VRES_LORA = 32
GATE_LORA = 128
RWKV_IN = 3 * RWKV_DIM + DECAY_LORA + ICLR_LORA + GATE_LORA
GN_EPS = 64e-5
SGU_CHUNK = 128
SGU_GROUPS = 4
SGU_DIM = 512
N_BRANCH = 3
EPS = 1e-6

LANES = 128
SUBLANES = 8
TOKEN_TILE = 256
RWKV_CHUNK = 64
MXU_DIM = 256
RWKV_PACK = MXU_DIM // RWKV_HEAD
VMEM_LIMIT = 56 * 1024 * 1024

BF16 = jnp.bfloat16
F32 = jnp.float32


def _cparams(sem):
    return pltpu.CompilerParams(dimension_semantics=sem, vmem_limit_bytes=VMEM_LIMIT)


def _dot(a, b):
    return jnp.dot(a.astype(BF16), b.astype(BF16), preferred_element_type=F32)


def _split2(x):
    hi = x.astype(BF16)
    lo = (x - hi.astype(F32)).astype(BF16)
    return hi, lo


def _head_sums(x, g_bf16):
    w = g_bf16.shape[0]
    hi, lo = _split2(x)
    parts = [jnp.dot(hi[:, j:j + w], g_bf16, preferred_element_type=F32)
             + jnp.dot(lo[:, j:j + w], g_bf16, preferred_element_type=F32) for j in range(0, x.shape[1], w)]
    return parts[0] if len(parts) == 1 else jnp.concatenate(parts, axis=1)


def _alternate(*gens):
    live = list(gens)
    while live:
        for g in list(live):
            try:
                next(g)
            except StopIteration:
                live.remove(g)


def _rms_mod(x, gain, shift, scale):
    ms = jnp.mean(x * x, axis=-1, keepdims=True)
    return (x * lax.rsqrt(ms + EPS) * gain) * (1.0 + scale) + shift


def _mod_kernel(c_ref, w_ref, b_ref, o_ref):
    c = c_ref[...]
    o_ref[...] = _dot(c * jax.nn.sigmoid(c), w_ref[...]) + b_ref[...]


def _modulation(cc, w_mod, b_mod):
    nl, d, n6 = w_mod.shape
    rows = cc.shape[0]
    tn = n6 // 4
    assert n6 % (4 * LANES) == 0
    return pl.pallas_call(
        _mod_kernel,
        grid=(nl, n6 // tn),
        in_specs=[
            pl.BlockSpec((rows, d), lambda l, j: (0, 0)),
            pl.BlockSpec((None, d, tn), lambda l, j: (l, 0, j)),
            pl.BlockSpec((None, 1, tn), lambda l, j: (l, 0, j)),
        ],
        out_specs=pl.BlockSpec((None, rows, tn), lambda l, j: (l, 0, j)),
        out_shape=jax.ShapeDtypeStruct((nl, rows, n6), F32),
        compiler_params=_cparams(("parallel", "parallel")),
        name="modulation",
    )(cc, w_mod, b_mod.reshape(nl, 1, n6))


def _stream_tile(n_ctx_tiles, x_refs):
    if len(x_refs) == 1:
        return x_refs[0][...]
    return jnp.where(pl.program_id(1) < n_ctx_tiles, x_refs[0][...], x_refs[1][...])


def _stream_specs(parts, tm, n_ctx_tiles, first_tile=0):
    d = parts[0].shape[-1]
    if len(parts) == 1:
        return [pl.BlockSpec((None, tm, d), lambda bi, i: (bi, i + first_tile, 0))]
    assert first_tile == 0
    return [pl.BlockSpec((None, tm, d), lambda bi, i: (bi, jnp.minimum(i, n_ctx_tiles - 1), 0)),
            pl.BlockSpec((None, tm, d), lambda bi, i: (bi, jnp.maximum(i - n_ctx_tiles, 0), 0))]


def _proj_in_kernel(has_vres, n_ctx_tiles, n_parts, *refs):
    x_refs, refs = refs[:n_parts], refs[n_parts:]
    (mod_ref, gain_ref, w_ref, cos_ref, sin_ref, qg_ref, kg_ref, gm_ref, lnw_ref, lnb_ref, sw_ref, sb_ref,
     q_ref, kv_ref, rw_ref, os_ref, *maybe_hd_ref) = refs
    x = _stream_tile(n_ctx_tiles, x_refs)
    mod = mod_ref[...]
    hb = _rms_mod(x, gain_ref[...], mod[0:1], mod[1:2]).astype(BF16)
    cos = cos_ref[...]
    sin = sin_ref[...]
    gm = gm_ref[...]

    def norm_rope(p, gain):
        reps = p.shape[1] // LANES
        tile = lambda a: a if reps == 1 else jnp.concatenate([a] * reps, axis=1)
        lane = lax.broadcasted_iota(jnp.int32, p.shape, 1)
        first_half = (lane % HEAD_DIM) < (HEAD_DIM // 2)
        ms = _head_sums(p * p, gm[:p.shape[1], :p.shape[1]])
        t = p * lax.rsqrt(ms + EPS) * tile(gain)
        swapped = jnp.where(first_half, pltpu.roll(t, p.shape[1] - HEAD_DIM // 2, 1), pltpu.roll(t, HEAD_DIM // 2, 1))
        return t * tile(cos) + swapped * tile(sin)

    proj = lambda c0, width: jnp.dot(hb, w_ref[:, c0:c0 + width], preferred_element_type=F32)
    c_q, c_kv = 0, ATT_Q_DIM
    c_rw = c_kv + 2 * ATT_KV_DIM
    c_sg = c_rw + RWKV_IN
    c_hd = c_sg + 2 * SGU_DIM
    gd = SGU_DIM // SGU_GROUPS

    def rwkv_columns():
        for j in range(0, RWKV_IN, MXU_DIM):
            rw_ref[:, j:j + MXU_DIM] = proj(c_rw + j, MXU_DIM)
            yield
        if has_vres:
            maybe_hd_ref[0][...] = proj(c_hd, LANES)
        yield

    def epilogues():
        pq = [proj(c_q + j, MXU_DIM) for j in range(0, ATT_Q_DIM, MXU_DIM)]
        pkv = proj(c_kv, 2 * ATT_KV_DIM)
        yield
        pu = proj(c_sg, SGU_DIM)
        for j, p in enumerate(pq):
            q_ref[:, j * MXU_DIM:(j + 1) * MXU_DIM] = norm_rope(p, qg_ref[...]).astype(BF16)
            yield
        pvg = proj(c_sg + SGU_DIM, SGU_DIM)
        kv_ref[:, 0:ATT_KV_DIM] = norm_rope(pkv[:, 0:ATT_KV_DIM], kg_ref[...]).astype(BF16)
        kv_ref[:, ATT_KV_DIM:2 * ATT_KV_DIM] = pkv[:, ATT_KV_DIM:2 * ATT_KV_DIM].astype(BF16)
        yield
        u = jax.nn.gelu(pu)
        yield
        vg = jax.nn.gelu(pvg)
        yield
        mu = jnp.mean(vg, axis=-1, keepdims=True)
        var = jnp.mean(jnp.square(vg - mu), axis=-1, keepdims=True)
        vn = ((vg - mu) * lax.rsqrt(var + EPS) * lnw_ref[...] + lnb_ref[...]).astype(BF16)
        yield
        for ck in range(x.shape[0] // SGU_CHUNK):
            rows = slice(ck * SGU_CHUNK, (ck + 1) * SGU_CHUNK)
            for g in range(SGU_GROUPS):
                cols = slice(g * gd, (g + 1) * gd)
                sp = jnp.dot(sw_ref[g], vn[rows, cols], preferred_element_type=F32) + sb_ref[g]
                os_ref[rows, cols] = (u[rows, cols] * sp).astype(BF16)
            yield

    _alternate(epilogues(), rwkv_columns())


def _proj_in(parts, modt, gain, w_ext, cos2, sin2, qg, kg, gmean, ln_w, ln_b, sgu_w, sgu_bb, n_ctx_tiles, has_vres):
    b, d = parts[0].shape[0], parts[0].shape[2]
    s = sum(p.shape[1] for p in parts)
    tm = TOKEN_TILE
    nt = s // tm
    tok = lambda width: pl.BlockSpec((None, tm, width), lambda bi, i: (bi, i, 0))
    full = lambda a: pl.BlockSpec(a.shape, lambda bi, i: (0,) * a.ndim)
    out_shapes = [
        jax.ShapeDtypeStruct((b, s, ATT_Q_DIM), BF16),
        jax.ShapeDtypeStruct((b, s, 2 * ATT_KV_DIM), BF16),
        jax.ShapeDtypeStruct((b, s, RWKV_IN), F32),
        jax.ShapeDtypeStruct((b, s, SGU_DIM), BF16),
    ]
    out_specs = [tok(ATT_Q_DIM), tok(2 * ATT_KV_DIM), tok(RWKV_IN), tok(SGU_DIM)]
    if has_vres:
        out_shapes.append(jax.ShapeDtypeStruct((b, s, LANES), F32))
        out_specs.append(tok(LANES))
    return pl.pallas_call(
        functools.partial(_proj_in_kernel, has_vres, n_ctx_tiles, len(parts)),
        grid=(b, nt),
        in_specs=_stream_specs(parts, tm, n_ctx_tiles) + [
            pl.BlockSpec((None, None, 6, d), lambda bi, i: (bi, jnp.where(i < n_ctx_tiles, 0, 1), 0, 0)),
            full(gain), full(w_ext),
            pl.BlockSpec((tm, LANES), lambda bi, i: (i, 0)),
            pl.BlockSpec((tm, LANES), lambda bi, i: (i, 0)),
            full(qg), full(kg), full(gmean), full(ln_w), full(ln_b), full(sgu_w), full(sgu_bb),
        ],
        out_specs=out_specs,
        out_shape=out_shapes,
        compiler_params=_cparams(("parallel", "parallel")),
        name="proj_in",
    )(*parts, modt, gain, w_ext, cos2, sin2, qg, kg, gmean, ln_w, ln_b, sgu_w, sgu_bb)


def _attn_kernel(n_ctx_blocks, n_blocks, first_step, sink_ref, q_ref, kvc_ref, kvp_ref, kvo_ref, kvn_ref, o_ref):
    blk = ATT_BLOCK
    cols = GQA_GROUP * blk
    nband = 3 * blk
    qq = lax.broadcasted_iota(jnp.int32, (nband, blk), 1)
    kcol = lax.broadcasted_iota(jnp.int32, (nband, blk), 0)
    kk = kcol % blk
    cgrp = lax.broadcasted_iota(jnp.int32, (1, cols), 1) // blk
    kvc = kvc_ref[...]
    kv_blocks = [kvp_ref[...]] + [kvo_ref[i * blk:(i + 1) * blk, :] for i in range(ATT_QBLOCKS)] + [kvn_ref[...]]
    v_t = lambda blkv: blkv[:, ATT_KV_DIM:2 * ATT_KV_DIM].astype(F32).T.astype(BF16)
    vts = []

    def value_prep():
        for blkv in [kvc] + kv_blocks:
            vts.append(v_t(blkv))
            yield

    nt_dims = (((1,), (1,)), ((), ()))
    o_t = [[None] * ATT_KV_HEADS for _ in range(ATT_QBLOCKS)]

    def chain(qb, kh):
        j = (pl.program_id(1) + first_step) * ATT_QBLOCKS + qb
        is_lat = j >= n_ctx_blocks
        ok_prev = j >= n_ctx_blocks + 1
        ok_next = jnp.logical_and(is_lat, j <= n_blocks - 2)
        off_prev = jnp.where(ok_prev, 0, blk)
        off_next = jnp.where(ok_next, 0, blk)
        own_end = jnp.where(is_lat, 2 * blk, 0)
        band_mask = (((kcol < blk) & (kk >= qq + off_prev))
                     | ((kcol >= blk) & (kcol < own_end))
                     | ((kcol >= 2 * blk) & (kk <= qq - off_next)))
        band_bias = jnp.where(band_mask, 0.0, -jnp.inf)
        band_bias = jnp.concatenate([band_bias] * GQA_GROUP, axis=1)
        h0 = kh * GQA_GROUP
        q = q_ref[qb * blk:(qb + 1) * blk, :] * ATT_SCALE
        qh = jnp.concatenate([q[:, (h0 + g) * HEAD_DIM:(h0 + g + 1) * HEAD_DIM] for g in range(GQA_GROUP)], axis=0)
        ksl = slice(kh * HEAD_DIM, (kh + 1) * HEAD_DIM)
        hsl = slice(kh * HEAD_DIM, (kh + 1) * HEAD_DIM)
        band = kv_blocks[qb:qb + 3]
        k_band = jnp.concatenate([blkv[:, ksl] for blkv in band], axis=0)
        s_ctx = lax.dot_general(kvc[:, ksl], qh, nt_dims, preferred_element_type=F32)
        s_band = lax.dot_general(k_band, qh, nt_dims, preferred_element_type=F32) + band_bias
        yield
        sink = jnp.zeros((1, cols), F32)
        for g in range(GQA_GROUP):
            sink = jnp.where(cgrp == g, sink_ref[h0 + g], sink)
        m = jnp.maximum(jnp.maximum(jnp.max(s_ctx, axis=0, keepdims=True), jnp.max(s_band, axis=0, keepdims=True)), sink)
        yield
        p_ctx = jnp.exp(s_ctx - m)
        yield
        p_band = jnp.exp(s_band - m)
        yield
        denom = (jnp.sum(p_ctx, axis=0, keepdims=True) + jnp.sum(p_band, axis=0, keepdims=True)) + jnp.exp(sink - m)
        yield
        vt_band = jnp.concatenate([vt[hsl, :] for vt in vts[1 + qb:4 + qb]], axis=1)
        o_t[qb][kh] = (jnp.dot(vts[0][hsl, :], p_ctx.astype(BF16), preferred_element_type=F32)
                       + jnp.dot(vt_band, p_band.astype(BF16), preferred_element_type=F32)) / denom
        yield

    _alternate(*[chain(qb, kh) for qb in range(ATT_QBLOCKS) for kh in range(ATT_KV_HEADS)], value_prep())
    for qb in range(ATT_QBLOCKS):
        both = jnp.concatenate(o_t[qb], axis=0)
        per_g = [both[:, g * blk:(g + 1) * blk].T for g in range(GQA_GROUP)]
        heads = [per_g[g][:, kh * HEAD_DIM:(kh + 1) * HEAD_DIM] for kh in range(ATT_KV_HEADS) for g in range(GQA_GROUP)]
        o_ref[qb * blk:(qb + 1) * blk, :] = jnp.concatenate(heads, axis=1).astype(BF16)


def _attention(q, kv, sink, n_ctx, first_block=0):
    b, s, _ = q.shape
    blk = ATT_BLOCK
    nb = s // blk
    ncb = n_ctx // blk
    kvw = kv.shape[-1]
    nq = ATT_QBLOCKS
    assert nb % nq == 0 and ncb % nq == 0 and first_block % nq == 0
    fs = first_block // nq
    return pl.pallas_call(
        functools.partial(_attn_kernel, ncb, nb, fs),
        grid=(b, nb // nq - fs),
        in_specs=[
            pl.BlockSpec(memory_space=pltpu.SMEM),
            pl.BlockSpec((None, nq * blk, ATT_Q_DIM), lambda bi, j: (bi, j + fs, 0)),
            pl.BlockSpec((None, n_ctx, kvw), lambda bi, j: (bi, 0, 0)),
            pl.BlockSpec((None, blk, kvw), lambda bi, j: (bi, jnp.maximum(nq * (j + fs) - 1, 0), 0)),
            pl.BlockSpec((None, nq * blk, kvw), lambda bi, j: (bi, j + fs, 0)),
            pl.BlockSpec((None, blk, kvw), lambda bi, j: (bi, jnp.minimum(nq * (j + fs) + nq, nb - 1), 0)),
        ],
        out_specs=pl.BlockSpec((None, nq * blk, ATT_Q_DIM), lambda bi, j: (bi, j, 0)),
        out_shape=jax.ShapeDtypeStruct((b, s - first_block * blk, ATT_Q_DIM), BF16),
        compiler_params=_cparams(("parallel", "parallel")),
        name="window_attention",
    )(sink, q, kv, kv, kv, kv)


def _rwkv_tile(d, i, n_tiles):
    return jnp.where(d == 0, i, jnp.where(i == 0, 0, n_tiles - i))


def _rwkv_ops_kernel(has_vres, emit_v, n_ctx_tiles, n_tiles, *refs):
    it = iter(refs)
    rw_ref, rwp_ref, rwn_ref = next(it), next(it), next(it)
    vf_ref = hd_ref = vup_ref = vb_ref = None
    if has_vres:
        vf_ref, hd_ref = next(it), next(it)
    conv_ref, wa_ref, w0a0_ref, kkw_ref, kaw_ref, rkw_ref, gup_ref = (next(it) for _ in range(7))
    if has_vres:
        vup_ref, vb_ref = next(it), next(it)
    gsum_ref, tri_ref, bdm_ref = next(it), next(it), next(it)
    rm_ref, yc_ref, aux_ref = next(it), next(it), next(it)
    v_out_ref = next(it) if emit_v else None
    fa_ref, fr_ref, fb_ref, fk_ref, fbh_ref, fkh_ref, fv_ref, fel_ref = (next(it) for _ in range(8))

    t = pl.program_id(1)
    tm = rw_ref.shape[0]
    c_dim = RWKV_DIM
    lc = RWKV_CHUNK
    n_chunks = tm // lc
    gw = RWKV_PACK * RWKV_HEAD
    n_groups = c_dim // gw

    x = rw_ref[...]
    ok_prev = jnp.logical_and(t != 0, t != n_ctx_tiles)
    ok_next = jnp.logical_and(t != n_ctx_tiles - 1, t != n_tiles - 1)
    halo_p = jnp.where(ok_prev, rwp_ref[SUBLANES - 1:SUBLANES, :], 0.0)
    halo_n = jnp.where(ok_next, rwn_ref[0:1, :], 0.0)
    edge = lax.broadcasted_iota(jnp.int32, (SUBLANES, x.shape[1]), 0)
    xp = pltpu.roll(x, 1, 0)
    xp = jnp.concatenate([jnp.where(edge == 0, halo_p, xp[0:SUBLANES]), xp[SUBLANES:]], axis=0)
    xn = pltpu.roll(x, tm - 1, 0)
    xn = jnp.concatenate([xn[0:tm - SUBLANES], jnp.where(edge == SUBLANES - 1, halo_n, xn[tm - SUBLANES:])], axis=0)
    cw = conv_ref[...]
    cv = (xp * cw[0:1] + x * cw[1:2]) + xn * cw[2:3]
    r = cv[:, 0:c_dim]
    k = cv[:, c_dim:2 * c_dim]
    v = cv[:, 2 * c_dim:3 * c_dim]
    xwa = cv[:, 3 * c_dim:3 * c_dim + LANES]
    xg = cv[:, 3 * c_dim + LANES:3 * c_dim + 2 * LANES]

    if emit_v:
        v_out_ref[...] = v
    if has_vres:
        gate = jax.nn.sigmoid(vb_ref[...] + _dot(hd_ref[...], vup_ref[...]))
        v = v + (vf_ref[...] - v) * gate
    fv_ref[...] = v

    gsum = gsum_ref[...]
    kk = k * kkw_ref[...]
    kk = kk * lax.rsqrt(_head_sums(kk * kk, gsum) + 1e-12)
    lane = lax.broadcasted_iota(jnp.int32, xwa.shape, 1)
    lora_in = jnp.where(lane < DECAY_LORA, jnp.tanh(xwa), xwa).astype(BF16)
    kd_sum = None
    for d in range(2):
        pre = jnp.dot(lora_in, wa_ref[d], preferred_element_type=F32) + w0a0_ref[d]
        a_sig = jax.nn.sigmoid(pre[:, c_dim:2 * c_dim])
        lw = -jnp.exp(-jax.nn.softplus(-pre[:, 0:c_dim]) - 0.5)
        kd = k * (1.0 + (a_sig - 1.0) * kaw_ref[...])
        kd_sum = kd if d == 0 else kd_sum + kd
        bvec = kk * a_sig
        hi, lo = _split2(lw)
        cs = jnp.dot(tri_ref[d], hi, preferred_element_type=F32) + jnp.dot(tri_ref[d], lo, preferred_element_type=F32)
        c_inc = cs[0:tm]
        c_tot = cs[tm:2 * tm]
        e_inc = jnp.exp(c_inc)
        e_neg = jnp.exp(-c_inc)
        e_rem = jnp.exp(c_tot - c_inc)
        fa_ref[d] = -kk * jnp.exp(c_inc - lw)
        fr_ref[d] = r * e_inc
        fb_ref[d] = bvec * e_neg
        fk_ref[d] = kd * e_neg
        fbh_ref[d] = bvec * e_rem
        fkh_ref[d] = kd * e_rem
        for c in range(n_chunks):
            fel_ref[d, c:c + 1, :] = jnp.exp(c_tot[c * lc:c * lc + 1, :])
    aux_ref[:, 0:c_dim] = _head_sums(r * kd_sum * rkw_ref[...], gsum) * v
    aux_ref[:, c_dim:2 * c_dim] = _dot(jax.nn.sigmoid(xg), gup_ref[...])

    bdm = bdm_ref[...]

    def bd(xb):
        return jnp.concatenate([xb] * RWKV_PACK, axis=0) * bdm

    low_half = lax.broadcasted_iota(jnp.int32, (lc, 2 * lc), 1) < lc

    def head_transpose(xf):
        w = jnp.concatenate([xf, xf], axis=0).T
        pairs = [jnp.where(low_half, w[(2 * j) * lc:(2 * j + 1) * lc], w[(2 * j + 1) * lc:(2 * j + 2) * lc])
                 for j in range(RWKV_PACK // 2)]
        return jnp.concatenate(pairs, axis=1)

    mm = lambda p, q: jnp.dot(p, q, preferred_element_type=F32)
    ri = lax.broadcasted_iota(jnp.int32, (2 * lc, 2 * gw), 0)
    ci = lax.broadcasted_iota(jnp.int32, (2 * lc, 2 * gw), 1) % lc
    amask = [jnp.logical_or(jnp.logical_and(ri < lc, sgn * (ri - ci) > 0),
                            jnp.logical_and(ri >= lc, sgn * (ri - lc - ci) >= 0)) for sgn in (1, -1)]
    eye_p = (lax.broadcasted_iota(jnp.int32, (lc, gw), 0)
             == lax.broadcasted_iota(jnp.int32, (lc, gw), 1) % lc).astype(F32)
    n_levels = int(math.log2(lc))

    probs = [(d, c, g) for d in range(2) for c in range(n_chunks) for g in range(n_groups)]
    blk = lambda ref, p: ref[p[0], p[1] * lc:(p[1] + 1) * lc, p[2] * gw:(p[2] + 1) * gw]
    vblk = lambda p: fv_ref[p[1] * lc:(p[1] + 1) * lc, p[2] * gw:(p[2] + 1) * gw]
    ab = [blk(fa_ref, p).astype(BF16) for p in probs]
    vb = [vblk(p).astype(BF16) for p in probs]
    a_all = []
    for j, p in enumerate(probs):
        lhs = jnp.concatenate([ab[j], blk(fr_ref, p).astype(BF16)], axis=0)
        rhs_t = jnp.concatenate([bd(blk(fb_ref, p).astype(BF16)), bd(blk(fk_ref, p).astype(BF16))], axis=0)
        full = lax.dot_general(lhs, rhs_t, (((1,), (1,)), ((), ())), preferred_element_type=F32)
        a_all.append(jnp.where(amask[p[0]], full, 0.0))
    a_rb = [m[lc:2 * lc, 0:gw].astype(BF16) for m in a_all]
    a_rk = [m[lc:2 * lc, gw:2 * gw].astype(BF16) for m in a_all]
    t_p = [eye_p + m[0:lc, 0:gw] for m in a_all]
    pb = [m[0:lc, 0:gw].astype(BF16) for m in a_all]
    pb = [mm(p, bd(p)).astype(BF16) for p in pb]
    for lvl in range(1, n_levels):
        if lvl < n_levels - 1:
            res = [mm(jnp.concatenate([p, t.astype(BF16)], axis=0), bd(p)) for p, t in zip(pb, t_p)]
            pb = [m[0:lc].astype(BF16) for m in res]
            t_p = [t + m[lc:2 * lc] for t, m in zip(t_p, res)]
        else:
            t_p = [t + mm(t.astype(BF16), bd(p)) for p, t in zip(pb, t_p)]
    u = [mm(m[0:lc, gw:2 * gw].astype(BF16), bd(vv)) for m, vv in zip(a_all, vb)]
    xs = [mm(t.astype(BF16), jnp.concatenate([bd(a), bd(uu.astype(BF16))], axis=1))
          for t, a, uu in zip(t_p, ab, u)]
    a1b = [m[:, 0:gw].astype(BF16) for m in xs]
    u1b = [m[:, gw:2 * gw].astype(BF16) for m in xs]
    for j, p in enumerate(probs):
        d, c, g = p
        zb = head_transpose(blk(fbh_ref, p)).astype(BF16)
        zk = head_transpose(blk(fkh_ref, p)).astype(BF16)
        res = mm(jnp.concatenate([a_rb[j], zb], axis=0), bd(a1b[j]))
        el_c = fel_ref[d, c:c + 1, g * gw:(g + 1) * gw]
        rm_ref[d, c, g] = jnp.concatenate([blk(fr_ref, p) + res[0:lc], res[lc:2 * lc] + eye_p * el_c],
                                          axis=0).astype(BF16)
        yc_ref[d, c, g] = mm(jnp.concatenate([jnp.concatenate([a_rb[j], a_rk[j]], axis=1),
                                              jnp.concatenate([zb, zk], axis=1)], axis=0),
                             jnp.concatenate([bd(u1b[j]), bd(vb[j])], axis=0))


def _rwkv_carry_kernel(rm_ref, yc_ref, bdm_ref, y_ref, st_ref):
    d = pl.program_id(0)
    i = pl.program_id(1)
    nb, n_chunks, n_groups, lc2, gw = rm_ref.shape
    lc = lc2 // 2
    bdm = bdm_ref[...]

    @pl.when(i == 0)
    def _():
        st_ref[...] = jnp.zeros_like(st_ref)

    for n in range(n_chunks):
        cc = jnp.where(d == 0, n, n_chunks - 1 - n)
        rows = pl.ds(pl.multiple_of(cc * lc, lc), lc)
        for bi in range(nb):
            for g in range(n_groups):
                st_bd = jnp.concatenate([st_ref[bi, g].astype(BF16)] * RWKV_PACK, axis=0) * bdm
                tot = jnp.dot(rm_ref[bi, cc, g], st_bd, preferred_element_type=F32) + yc_ref[bi, cc, g]
                y_ref[bi, rows, g * gw:(g + 1) * gw] = tot[0:lc]
                st_ref[bi, g] = tot[lc:2 * lc]


def _rwkv(rw, conv_w, wa, w0a0, k_k, k_a, r_k, g_up, gsum, tri, bdm, n_ctx_tiles, emit_v, vres=None):
    b, s, _ = rw.shape
    tm = TOKEN_TILE
    nt = s // tm
    c = RWKV_DIM
    lc = RWKV_CHUNK
    gw = RWKV_PACK * RWKV_HEAD
    n_groups, n_chunks = c // gw, tm // lc
    halo_blocks = tm // SUBLANES
    last_halo = s // SUBLANES - 1
    tok = lambda width: pl.BlockSpec((None, tm, width), lambda bi, t: (bi, t, 0))
    full = lambda a: pl.BlockSpec(a.shape, lambda bi, t: (0,) * a.ndim)
    in_specs = [
        tok(RWKV_IN),
        pl.BlockSpec((None, SUBLANES, RWKV_IN), lambda bi, t: (bi, jnp.maximum(t * halo_blocks - 1, 0), 0)),
        pl.BlockSpec((None, SUBLANES, RWKV_IN), lambda bi, t: (bi, jnp.minimum((t + 1) * halo_blocks, last_halo), 0)),
    ]
    args = [rw, rw, rw]
    if vres is not None:
        v_first, hd, vup, vb = vres
        in_specs += [tok(c), tok(LANES)]
        args += [v_first, hd]
    in_specs += [full(conv_w), full(wa), full(w0a0), full(k_k), full(k_a), full(r_k), full(g_up)]
    args += [conv_w, wa, w0a0, k_k, k_a, r_k, g_up]
    if vres is not None:
        in_specs += [full(vup), full(vb)]
        args += [vup, vb]
    in_specs += [full(gsum), full(tri), full(bdm)]
    args += [gsum, tri, bdm]
    op_shape = (2, b, nt, n_chunks, n_groups, 2 * lc, gw)
    op_spec = pl.BlockSpec((2, None, None, n_chunks, n_groups, 2 * lc, gw), lambda bi, t: (0, bi, t, 0, 0, 0, 0))
    out_specs = [op_spec, op_spec, tok(2 * c)]
    out_shape = [jax.ShapeDtypeStruct(op_shape, BF16), jax.ShapeDtypeStruct(op_shape, F32),
                 jax.ShapeDtypeStruct((b, s, 2 * c), F32)]
    if emit_v:
        out_specs.append(tok(c))
        out_shape.append(jax.ShapeDtypeStruct((b, s, c), F32))
    feat_buf = pltpu.VMEM((2, tm, c), F32)
    scratch = [feat_buf] * 6 + [pltpu.VMEM((tm, c), F32), pltpu.VMEM((2, SUBLANES * (-(-n_chunks // SUBLANES)), c), F32)]
    outs = pl.pallas_call(
        functools.partial(_rwkv_ops_kernel, vres is not None, emit_v, n_ctx_tiles, nt),
        grid=(b, nt),
        in_specs=in_specs,
        out_specs=out_specs,
        out_shape=out_shape,
        scratch_shapes=scratch,
        compiler_params=_cparams(("parallel", "parallel")),
        name="rwkv7_chunk_ops",
    )(*args)
    rm, yc = outs[0], outs[1]

    tile = lambda d, i: _rwkv_tile(d, i, nt)
    carry_in = pl.BlockSpec((None, b, None, n_chunks, n_groups, 2 * lc, gw), lambda d, i: (d, 0, tile(d, i), 0, 0, 0, 0))
    y = pl.pallas_call(
        _rwkv_carry_kernel,
        grid=(2, nt),
        in_specs=[carry_in, carry_in, pl.BlockSpec(bdm.shape, lambda d, i: (0, 0))],
        out_specs=pl.BlockSpec((None, b, tm, c), lambda d, i: (d, 0, tile(d, i), 0)),
        out_shape=jax.ShapeDtypeStruct((2, b, s, c), F32),
        scratch_shapes=[pltpu.VMEM((b, n_groups, lc, gw), F32)],
        compiler_params=_cparams(("parallel", "arbitrary")),
        name="rwkv7_state_carry",
    )(rm, yc, bdm)
    return (y,) + tuple(outs[2:])


def _merge_kernel(n_ctx_tiles, n_parts, *refs):
    x_refs, refs = refs[:n_parts], refs[n_parts:]
    (mod_ref, gain_ref, wgt_ref, oa_ref, y_ref, aux_ref, os_ref, lnw_ref, lnb_ref, gm_ref,
     woa_ref, wor_ref, wos_ref, wout_ref, o_ref) = refs
    d_model = o_ref.shape[-1]
    c = RWKV_DIM
    x = _stream_tile(n_ctx_tiles, x_refs)
    mod = mod_ref[...]
    hb = _rms_mod(x, gain_ref[...], mod[0:1], mod[1:2]).astype(BF16)
    gm = gm_ref[...]
    gates, o_rwkv = [], []

    def gate_stream():
        for j in range(N_BRANCH):
            gates.append(jax.nn.sigmoid(jnp.dot(hb, wgt_ref[:, j * d_model:(j + 1) * d_model],
                                                preferred_element_type=F32)))
            yield

    def rwkv_epilogue():
        y = y_ref[0] + y_ref[1]
        mu = _head_sums(y, gm)
        yield
        yc = y - mu
        var = _head_sums(yc * yc, gm)
        yield
        yn = yc * lax.rsqrt(var + GN_EPS) * lnw_ref[...] + lnb_ref[...]
        o_rwkv.append(((yn + aux_ref[:, 0:c]) * aux_ref[:, c:2 * c]).astype(BF16))
        yield

    _alternate(gate_stream(), rwkv_epilogue())
    m = (gates[0] * jnp.dot(oa_ref[...], woa_ref[...], preferred_element_type=F32)
         + gates[1] * jnp.dot(o_rwkv[0], wor_ref[...], preferred_element_type=F32)
         + gates[2] * jnp.dot(os_ref[...], wos_ref[...], preferred_element_type=F32))
    mix = _dot(m, wout_ref[...])
    o_ref[...] = x + mod[2:3] * mix


def _merge(parts, modt, gain, w_gt, o_attn, y, aux, o_sgu, ln_w, ln_b, gmean, woa, wor, wos, wout, n_ctx_tiles,
           first_tile=0):
    b, d = parts[0].shape[0], parts[0].shape[2]
    s = sum(p.shape[1] for p in parts)
    tm = TOKEN_TILE
    nt = s // tm - first_tile
    assert o_attn.shape[1] == nt * tm
    tok = lambda width: pl.BlockSpec((None, tm, width), lambda bi, i: (bi, i + first_tile, 0))
    tok2 = lambda width: pl.BlockSpec((2, None, tm, width), lambda bi, i: (0, bi, i + first_tile, 0))
    own = lambda width: pl.BlockSpec((None, tm, width), lambda bi, i: (bi, i, 0))
    full = lambda a: pl.BlockSpec(a.shape, lambda bi, i: (0,) * a.ndim)
    return pl.pallas_call(
        functools.partial(_merge_kernel, n_ctx_tiles, len(parts)),
        grid=(b, nt),
        in_specs=_stream_specs(parts, tm, n_ctx_tiles, first_tile) + [
            pl.BlockSpec((None, None, 6, d), lambda bi, i: (bi, jnp.where(i + first_tile < n_ctx_tiles, 0, 1), 0, 0)),
            full(gain), full(w_gt),
            own(ATT_Q_DIM), tok2(RWKV_DIM), tok(2 * RWKV_DIM), tok(SGU_DIM),
            full(ln_w), full(ln_b), full(gmean), full(woa), full(wor), full(wos), full(wout),
        ],
        out_specs=own(d),
        out_shape=jax.ShapeDtypeStruct((b, nt * tm, d), F32),
        compiler_params=_cparams(("parallel", "parallel")),
        name="merge_out",
    )(*parts, modt, gain, w_gt, o_attn, y, aux, o_sgu, ln_w, ln_b, gmean, woa, wor, wos, wout)


def _ffn_kernel(x_ref, mod_ref, gain_ref, w1_ref, w3_ref, w2_ref, o_ref):
    x = x_ref[...]
    mod = mod_ref[...]
    hb = _rms_mod(x, gain_ref[...], mod[3:4], mod[4:5]).astype(BF16)
    a = jnp.dot(hb, w1_ref[...], preferred_element_type=F32)
    bq = jnp.dot(hb, w3_ref[...], preferred_element_type=F32)
    hid = (a * jax.nn.sigmoid(a)) * bq
    o_ref[...] = x + mod[5:6] * _dot(hid, w2_ref[...])


def _ffn(xs, modt, gain, w1, w3, w2, n_ctx_tiles, first_tile):
    b, s, d = xs.shape
    tm = TOKEN_TILE
    nt = s // tm - first_tile
    full = lambda a: pl.BlockSpec(a.shape, lambda bi, i: (0,) * a.ndim)
    return pl.pallas_call(
        _ffn_kernel,
        grid=(b, nt),
        in_specs=[
            pl.BlockSpec((None, tm, d), lambda bi, i: (bi, i + first_tile, 0)),
            pl.BlockSpec((None, None, 6, d), lambda bi, i: (bi, jnp.where(i + first_tile < n_ctx_tiles, 0, 1), 0, 0)),
            full(gain), full(w1), full(w3), full(w2),
        ],
        out_specs=pl.BlockSpec((None, tm, d), lambda bi, i: (bi, i, 0)),
        out_shape=jax.ShapeDtypeStruct((b, nt * tm, d), F32),
        compiler_params=_cparams(("parallel", "parallel")),
        name="swiglu_ffn",
    )(xs, modt, gain, w1, w3, w2)


def _rope_tables(n_ctx, n_lat):
    rows = n_lat // GRID_W
    row = jnp.repeat(jnp.arange(rows), GRID_W).astype(F32)
    col = jnp.tile(jnp.arange(GRID_W), rows).astype(F32)
    inv = ROPE_BASE ** (-jnp.arange(ROPE_FREQS, dtype=F32) / ROPE_FREQS)
    ang = jnp.concatenate([row[:, None] * inv, col[:, None] * inv], -1)
    cos, sin = jnp.cos(ang), jnp.sin(ang)
    cos = jnp.concatenate([jnp.ones((n_ctx, HEAD_DIM // 2), F32), cos], 0)
    sin = jnp.concatenate([jnp.zeros((n_ctx, HEAD_DIM // 2), F32), sin], 0)
    return jnp.tile(jnp.concatenate([cos, cos], -1), (1, 2)), jnp.tile(jnp.concatenate([-sin, sin], -1), (1, 2))


def _block_diag_ones(n, blk, scale):
    i = np.arange(n)
    return jnp.asarray(((i[:, None] // blk) == (i[None, :] // blk)).astype(np.float32) * scale, dtype=BF16)


def _cumsum_matrices(tm, lc):
    i = np.arange(tm)
    same = (i[:, None] // lc) == (i[None, :] // lc)
    fwd = same & (i[None, :] <= i[:, None])
    bwd = same & (i[None, :] >= i[:, None])
    mats = np.stack([np.concatenate([fwd, same], 0), np.concatenate([bwd, same], 0)]).astype(np.float32)
    return jnp.asarray(mats, dtype=BF16)


def kernel(x, c, ctx, c_ctx, w_mod, b_mod, norm_mix, norm_ffn, w_in, q_gain, k_gain, attn_sink,
           rwkv_conv, rwkv_w0, rwkv_w_up, rwkv_a0, rwkv_a_up, rwkv_k_k, rwkv_k_a, rwkv_r_k, rwkv_g_up,
           rwkv_ln_w, rwkv_ln_b, rwkv_vres_down, rwkv_vres_up, rwkv_vres_b,
           sgu_ln_w, sgu_ln_b, sgu_w, sgu_b, w_o_attn, w_o_rwkv, w_o_sgu, w_out,
           ffn_w1, ffn_w3, ffn_w2):
    b, n_lat, d = x.shape
    n_ctx = ctx.shape[1]
    depth = w_in.shape[0]
    tm = TOKEN_TILE
    assert n_ctx % tm == 0 and n_lat % tm == 0 and n_lat % GRID_W == 0 and 2 * RWKV_CHUNK == LANES
    n_ctx_tiles = n_ctx // tm
    cdim = RWKV_DIM

    stream = (ctx, x)
    cos2, sin2 = _rope_tables(n_ctx, n_lat)
    gmean = _block_diag_ones(MXU_DIM, HEAD_DIM, 1.0 / HEAD_DIM)
    gsum = _block_diag_ones(MXU_DIM, RWKV_HEAD, 1.0)
    tri = _cumsum_matrices(tm, RWKV_CHUNK)
    bdm = _block_diag_ones(RWKV_PACK * RWKV_HEAD, RWKV_HEAD, 1.0)

    rows = -(-(b + 1) // SUBLANES) * SUBLANES
    cc = jnp.zeros((rows, d), F32).at[:b].set(c).at[b].set(c_ctx)
    mods = _modulation(cc, w_mod, b_mod)

    row2 = lambda a: a.reshape(1, -1)
    v_first = None
    for l in range(depth):
        last = l == depth - 1
        has_vres = l > 0
        mod_lat = mods[l, :b].reshape(b, 1, 6, d)
        mod_ctx = jnp.broadcast_to(mods[l, b].reshape(1, 1, 6, d), (b, 1, 6, d))
        modt = jnp.concatenate([mod_ctx, mod_lat], axis=1)

        n_proj = w_in.shape[2] - N_BRANCH * d
        w_ext = w_in[l, :, :n_proj]
        if has_vres:
            down = jnp.pad(rwkv_vres_down[l - 1], ((0, 0), (0, LANES - VRES_LORA)))
            w_ext = jnp.concatenate([w_ext, down], axis=1)
        w_ext = w_ext.astype(BF16)
        w_gt = w_in[l, :, n_proj:].astype(BF16)
        qg2 = jnp.tile(q_gain[l], 2).reshape(1, LANES)
        kg2 = jnp.tile(k_gain[l], 2).reshape(1, LANES)
        sgu_bb = jnp.broadcast_to(sgu_b[l][:, :, None], (SGU_GROUPS, SGU_CHUNK, SGU_DIM // SGU_GROUPS))
        outs = _proj_in(stream, modt, row2(norm_mix[l]), w_ext, cos2, sin2, qg2, kg2, gmean,
                        row2(sgu_ln_w[l]), row2(sgu_ln_b[l]), sgu_w[l].astype(BF16), sgu_bb, n_ctx_tiles, has_vres)
        q, kv, rw, o_sgu = outs[:4]

        skip = n_ctx_tiles if last and len(stream) == 1 else 0
        o_attn = _attention(q, kv, attn_sink[l], n_ctx, first_block=skip * (tm // ATT_BLOCK))

        zeros_lora = jnp.zeros((2, DECAY_LORA, cdim), F32)
        wa = jnp.concatenate([jnp.concatenate([rwkv_w_up[l], zeros_lora], axis=2),
                              jnp.concatenate([zeros_lora, rwkv_a_up[l]], axis=2)], axis=1).astype(BF16)
        w0a0 = jnp.concatenate([rwkv_w0[l], rwkv_a0[l]], axis=1).reshape(2, 1, 2 * cdim)
        vres = None
        if has_vres:
            vup = jnp.pad(rwkv_vres_up[l - 1], ((0, LANES - VRES_LORA), (0, 0))).astype(BF16)
            vres = (v_first, outs[4], vup, row2(rwkv_vres_b[l - 1]))
        r_outs = _rwkv(rw, rwkv_conv[l], wa, w0a0, row2(rwkv_k_k[l]), row2(rwkv_k_a[l]), row2(rwkv_r_k[l]),
                       rwkv_g_up[l].astype(BF16), gsum, tri, bdm, n_ctx_tiles, emit_v=(l == 0), vres=vres)
        y, aux = r_outs[0], r_outs[1]
        if l == 0:
            v_first = r_outs[2]

        xs = _merge(stream, modt, row2(norm_mix[l]), w_gt, o_attn, y, aux, o_sgu, row2(rwkv_ln_w[l]), row2(rwkv_ln_b[l]),
                    gmean, w_o_attn[l].astype(BF16), w_o_rwkv[l].astype(BF16), w_o_sgu[l].astype(BF16),
                    w_out[l].astype(BF16), n_ctx_tiles, first_tile=skip)
        xs = _ffn(xs, modt, row2(norm_ffn[l]), ffn_w1[l].astype(BF16), ffn_w3[l].astype(BF16), ffn_w2[l].astype(BF16),
                  n_ctx_tiles - skip, first_tile=n_ctx_tiles - skip if last else 0)
        stream = (xs,)
    return xs
```
